```python
import jax, jax.numpy as jnp
from jax import lax
import numpy as np

D_MODEL = 1024
BATCH = 2
SEQ = 8192
DEPTH = 1
DEC_BATCH = 128
DEC_SEQ = 4
PAST_LEN = 16384
PAGE_SIZE = 128

N_META = 16
SB_HEADS = 8
SB_HEAD_DIM = 64
SB_WIDTH = SB_HEADS * SB_HEAD_DIM
MLA_HEADS = 8
MLA_NOPE_DIM = 64
MLA_ROPE_DIM = 32
MLA_V_DIM = 64
MLA_WIDTH = MLA_HEADS * MLA_V_DIM
MLA_Q_RANK = 256
MLA_KV_RANK = 128
MIX_WIDTH = SB_WIDTH + MLA_WIDTH
D_FF = 2816
Q_BLOCK = 128
ROPE_BASE = 10000.0
LN_EPS = 1e-5
RMS_EPS = 1e-6
POOL_NUM = 5
POOL_DEN = 4
DEEPNORM_ALPHA = (2 * DEPTH) ** 0.25
DEEPNORM_BETA = (8 * DEPTH) ** -0.25
SB_SCALE = SB_HEAD_DIM ** -0.5
MLA_SCALE = (MLA_NOPE_DIM + MLA_ROPE_DIM) ** -0.5
IN_SPLITS = (SB_WIDTH, 2 * SB_WIDTH, 3 * SB_WIDTH,
             3 * SB_WIDTH + MLA_Q_RANK, 3 * SB_WIDTH + MLA_Q_RANK + MLA_KV_RANK)
IN_WIDTH = IN_SPLITS[-1] + MLA_ROPE_DIM

kernel_name = "hymba_stickbreaking_mla_decode_step"


def layer_norm(x, g, b):
    xf = x.astype(jnp.float32)
    mu = jnp.mean(xf, -1, keepdims=True)
    var = jnp.mean(jnp.square(xf - mu), -1, keepdims=True)
    return ((xf - mu) * lax.rsqrt(var + LN_EPS) * g + b).astype(x.dtype)


def rms_norm(x, g):
    xf = x.astype(jnp.float32)
    return (xf * lax.rsqrt(jnp.mean(xf * xf, -1, keepdims=True) + RMS_EPS) * g).astype(x.dtype)


def swiglu(x, w_gu, w_down):
    gate, up = jnp.split(x @ w_gu, 2, axis=-1)
    return (jax.nn.silu(gate) * up) @ w_down


def half_ffn_block(x, w_gu, w_down, g, b):
    return layer_norm(DEEPNORM_ALPHA * x + 0.5 * swiglu(x, w_gu, w_down), g, b)


def rope_angles(pos):
    inv = ROPE_BASE ** (-jnp.arange(0, MLA_ROPE_DIM, 2, dtype=jnp.float32) / MLA_ROPE_DIM)
    ang = pos.astype(jnp.float32)[:, None] * inv[None, :]
    return jnp.cos(ang), jnp.sin(ang)


def apply_rope(x, cos, sin):
    x1, x2 = jnp.split(x.astype(jnp.float32), 2, axis=-1)
    return jnp.concatenate([x1 * cos - x2 * sin, x1 * sin + x2 * cos], -1).astype(x.dtype)


def mixer_inputs(h, cos, sin, w_in, q_norm_g, w_uq, kv_norm_g, w_uk):
    B, T = h.shape[:2]
    sq, sk, sv, cq, ckv, kr = jnp.split(h @ w_in, IN_SPLITS, axis=-1)
    shp = (B, T, SB_HEADS, SB_HEAD_DIM)
    q = (rms_norm(cq, q_norm_g) @ w_uq).reshape(B, T, MLA_HEADS, MLA_NOPE_DIM + MLA_ROPE_DIM)
    q_nope, q_rope = q[..., :MLA_NOPE_DIM], q[..., MLA_NOPE_DIM:]
    q_lat = jnp.einsum('bthn,chn->bthc', q_nope, w_uk)
    q_rope = apply_rope(q_rope, cos[:, None, :], sin[:, None, :])
    ckv = rms_norm(ckv, kv_norm_g)
    kr = apply_rope(kr, cos, sin)
    return sq.reshape(shp), sk.reshape(shp), sv.reshape(shp), q_lat, q_rope, ckv, kr


def stick_breaking(q, k, v, q_pos, k_pos):
    z = jnp.einsum('bqhe,bkhe->bhqk', q, k).astype(jnp.float32) * SB_SCALE
    visible = k_pos[None, :] < q_pos[:, None]
    log_beta = jax.nn.log_sigmoid(z)
    log_keep = jnp.where(visible, log_beta - z, 0.0)
    after = lax.cumsum(log_keep, axis=3, reverse=True) - log_keep
    w = jnp.where(visible, jnp.exp(log_beta + after), 0.0)
    return jnp.einsum('bhqk,bkhe->bqhe', w.astype(v.dtype), v)


def latent_attention(q_lat, q_rope, ckv, kr, q_pos, k_pos, w_uv):
    s = (jnp.einsum('bqhc,bkc->bhqk', q_lat, ckv)
         + jnp.einsum('bqhr,bkr->bhqk', q_rope, kr)).astype(jnp.float32) * MLA_SCALE
    s = jnp.where(k_pos[None, :] <= q_pos[:, None], s, -jnp.inf)
    p = jax.nn.softmax(s, axis=-1)
    o_lat = jnp.einsum('bhqk,bkc->bqhc', p.astype(ckv.dtype), ckv)
    return jnp.einsum('bqhc,chv->bqhv', o_lat, w_uv)


def prompt_mixers(sq, sk, sv, q_lat, q_rope, ckv, kr, w_uv):
    B, L = sq.shape[:2]
    n_blk = -(-L // Q_BLOCK)
    L_pad = n_blk * Q_BLOCK
    pad = lambda a: jnp.pad(a, [(0, 0), (0, L_pad - L)] + [(0, 0)] * (a.ndim - 2))
    sq, sk, sv, q_lat, q_rope, ckv, kr = [pad(a) for a in (sq, sk, sv, q_lat, q_rope, ckv, kr)]
    k_pos = jnp.arange(L_pad)

    def block(i):
        start = i * Q_BLOCK
        take = lambda a: lax.dynamic_slice_in_dim(a, start, Q_BLOCK, axis=1)
        q_pos = start + jnp.arange(Q_BLOCK)
        o_sb = stick_breaking(take(sq), sk, sv, q_pos, k_pos)
        o_mla = latent_attention(take(q_lat), take(q_rope), ckv, kr, q_pos, k_pos, w_uv)
        return o_sb, o_mla

    o_sb, o_mla = lax.map(block, jnp.arange(n_blk))
    unblock = lambda o: jnp.moveaxis(o, 0, 1).reshape((B, L_pad) + o.shape[3:])[:, :L]
    return unblock(o_sb), unblock(o_mla)


def sample_mixers(layer, sq, sk, sv, q_lat, q_rope, ckv, kr,
                  pool_k, pool_v, pool_c, pool_r, page_table, w_uv):
    past = page_table.shape[1] * PAGE_SIZE
    T = sq.shape[1]
    q_pos = past + jnp.arange(T)
    k_pos = jnp.arange(past + T)

    def one_sequence(args):
        pages, q1, k1, v1, ql1, qr1, c1, r1 = args
        cat = lambda pool, new: jnp.concatenate(
            [pool[layer, pages].reshape((past,) + pool.shape[3:]), new], axis=0)[None]
        o_sb = stick_breaking(q1[None], cat(pool_k, k1), cat(pool_v, v1), q_pos, k_pos)
        o_mla = latent_attention(ql1[None], qr1[None], cat(pool_c, c1), cat(pool_r, r1),
                                 q_pos, k_pos, w_uv)
        return o_sb[0], o_mla[0]

    return lax.map(one_sequence, (page_table, sq, sk, sv, q_lat, q_rope, ckv, kr))


def mix_block(h, o_sb, o_mla, sb_out_g, mla_out_g, w_o, g, b):
    B, T = h.shape[:2]
    o = jnp.concatenate([rms_norm(o_sb.reshape(B, T, SB_WIDTH), sb_out_g),
                         rms_norm(o_mla.reshape(B, T, MLA_WIDTH), mla_out_g)], axis=-1)
    return layer_norm(DEEPNORM_ALPHA * h + o @ w_o, g, b)


def setup_inputs(seed: int = 0) -> dict:
    key = jax.random.key(seed)
    ks = jax.random.split(key, 32)
    n_pages = PAST_LEN // PAGE_SIZE
    n_used = DEC_BATCH * n_pages
    n_pool = (n_used * POOL_NUM) // POOL_DEN
    f32 = jnp.float32
    nrm = lambda k, shape, scale: jax.random.normal(k, shape, f32) * scale
    gain = lambda k, n: 1.0 + 0.05 * jax.random.normal(k, (DEPTH, n), f32)
    bias = lambda k, n: 0.02 * jax.random.normal(k, (DEPTH, n), f32)
    col_scale = jnp.ones((IN_WIDTH,), f32).at[2 * SB_WIDTH:3 * SB_WIDTH].set(DEEPNORM_BETA)
    page_table = jax.random.permutation(ks[6], n_pool)[:n_used].reshape(DEC_BATCH, n_pages).astype(jnp.int32)
    return {
        "x_prompt": jax.random.normal(ks[0], (BATCH, SEQ, D_MODEL), f32),
        "x_sample": jax.random.normal(ks[1], (DEC_BATCH, DEC_SEQ, D_MODEL), f32),
        "cache_sb_k": jax.random.normal(ks[2], (DEPTH, n_pool, PAGE_SIZE, SB_HEADS, SB_HEAD_DIM), f32),
        "cache_sb_v": jax.random.normal(ks[3], (DEPTH, n_pool, PAGE_SIZE, SB_HEADS, SB_HEAD_DIM), f32),
        "cache_mla_ckv": jax.random.normal(ks[4], (DEPTH, n_pool, PAGE_SIZE, MLA_KV_RANK), f32),
        "cache_mla_krope": jax.random.normal(ks[5], (DEPTH, n_pool, PAGE_SIZE, MLA_ROPE_DIM), f32),
        "page_table": page_table,
        "meta_tokens": jax.random.normal(ks[7], (N_META, D_MODEL), f32),
        "ffn1_w_gu": nrm(ks[8], (DEPTH, D_MODEL, 2 * D_FF), D_MODEL ** -0.5),
        "ffn1_w_down": nrm(ks[9], (DEPTH, D_FF, D_MODEL), D_FF ** -0.5 * DEEPNORM_BETA),
        "ln1_g": gain(ks[10], D_MODEL),
        "ln1_b": bias(ks[11], D_MODEL),
        "w_in": nrm(ks[12], (DEPTH, D_MODEL, IN_WIDTH), D_MODEL ** -0.5) * col_scale,
        "q_norm_g": gain(ks[13], MLA_Q_RANK),
        "w_uq": nrm(ks[14], (DEPTH, MLA_Q_RANK, MLA_HEADS * (MLA_NOPE_DIM + MLA_ROPE_DIM)), MLA_Q_RANK ** -0.5),
        "kv_norm_g": gain(ks[15], MLA_KV_RANK),
        "w_uk": nrm(ks[16], (DEPTH, MLA_KV_RANK, MLA_HEADS, MLA_NOPE_DIM), MLA_KV_RANK ** -0.5),
        "w_uv": nrm(ks[17], (DEPTH, MLA_KV_RANK, MLA_HEADS, MLA_V_DIM), MLA_KV_RANK ** -0.5 * DEEPNORM_BETA),
        "sb_out_g": gain(ks[18], SB_WIDTH),
        "mla_out_g": gain(ks[19], MLA_WIDTH),
        "w_o": nrm(ks[20], (DEPTH, MIX_WIDTH, D_MODEL), MIX_WIDTH ** -0.5 * DEEPNORM_BETA),
        "ln2_g": gain(ks[21], D_MODEL),
        "ln2_b": bias(ks[22], D_MODEL),
        "ffn2_w_gu": nrm(ks[23], (DEPTH, D_MODEL, 2 * D_FF), D_MODEL ** -0.5),
        "ffn2_w_down": nrm(ks[24], (DEPTH, D_FF, D_MODEL), D_FF ** -0.5 * DEEPNORM_BETA),
        "ln3_g": gain(ks[25], D_MODEL),
        "ln3_b": bias(ks[26], D_MODEL),
    }


def reference(x_prompt, x_sample, cache_sb_k, cache_sb_v, cache_mla_ckv, cache_mla_krope, page_table,
              meta_tokens, ffn1_w_gu, ffn1_w_down, ln1_g, ln1_b, w_in, q_norm_g, w_uq, kv_norm_g,
              w_uk, w_uv, sb_out_g, mla_out_g, w_o, ln2_g, ln2_b, ffn2_w_gu, ffn2_w_down, ln3_g, ln3_b):
    L = SEQ + N_META
    cos_p, sin_p = rope_angles(jnp.arange(L))
    cos_s, sin_s = rope_angles(PAST_LEN + jnp.arange(DEC_SEQ))
    meta = jnp.broadcast_to(meta_tokens[None].astype(x_prompt.dtype), (BATCH, N_META, D_MODEL))
    hp = jnp.concatenate([meta, x_prompt], axis=1)
    hs = x_sample
    kp_l, vp_l, cp_l, rp_l = [], [], [], []
    ks_l, vs_l, cs_l, rs_l = [], [], [], []
    for l in range(DEPTH):
        hp = half_ffn_block(hp, ffn1_w_gu[l], ffn1_w_down[l], ln1_g[l], ln1_b[l])
        hs = half_ffn_block(hs, ffn1_w_gu[l], ffn1_w_down[l], ln1_g[l], ln1_b[l])
        sq, sk, sv, ql, qr, ckv, kr = mixer_inputs(hp, cos_p, sin_p, w_in[l], q_norm_g[l], w_uq[l],
                                                   kv_norm_g[l], w_uk[l])
        o_sb, o_mla = prompt_mixers(sq, sk, sv, ql, qr, ckv, kr, w_uv[l])
        hp = mix_block(hp, o_sb, o_mla, sb_out_g[l], mla_out_g[l], w_o[l], ln2_g[l], ln2_b[l])
        kp_l.append(sk); vp_l.append(sv); cp_l.append(ckv); rp_l.append(kr)
        sq, sk, sv, ql, qr, ckv, kr = mixer_inputs(hs, cos_s, sin_s, w_in[l], q_norm_g[l], w_uq[l],
                                                   kv_norm_g[l], w_uk[l])
        o_sb, o_mla = sample_mixers(l, sq, sk, sv, ql, qr, ckv, kr, cache_sb_k, cache_sb_v,
                                    cache_mla_ckv, cache_mla_krope, page_table, w_uv[l])
        hs = mix_block(hs, o_sb, o_mla, sb_out_g[l], mla_out_g[l], w_o[l], ln2_g[l], ln2_b[l])
        ks_l.append(sk); vs_l.append(sv); cs_l.append(ckv); rs_l.append(kr)
        hp = half_ffn_block(hp, ffn2_w_gu[l], ffn2_w_down[l], ln3_g[l], ln3_b[l])
        hs = half_ffn_block(hs, ffn2_w_gu[l], ffn2_w_down[l], ln3_g[l], ln3_b[l])
    y_prompt = hp[:, N_META:]
    y_sample = hs
    return (y_prompt, y_sample,
            jnp.stack(kp_l), jnp.stack(vp_l), jnp.stack(cp_l), jnp.stack(rp_l),
            jnp.stack(ks_l), jnp.stack(vs_l), jnp.stack(cs_l), jnp.stack(rs_l))
```

```python
import functools

import jax
import jax.numpy as jnp
from jax import lax
from jax.experimental import pallas as pl
from jax.experimental.pallas import tpu as pltpu

F32 = jnp.float32
BF16 = jnp.bfloat16

D_MODEL = 1024
N_META = 16
HEADS = 8
SB_HEAD_DIM = 64
SB_WIDTH = HEADS * SB_HEAD_DIM
MLA_NOPE_DIM = 64
MLA_ROPE_DIM = 32
MLA_V_DIM = 64
MLA_WIDTH = HEADS * MLA_V_DIM
MLA_Q_RANK = 256
MLA_KV_RANK = 128
D_FF = 2816
PAGE_SIZE = 128
ROPE_BASE = 10000.0
LN_EPS = 1e-5
RMS_EPS = 1e-6
DEPTH = 1
DEEPNORM_ALPHA = (2 * DEPTH) ** 0.25
SB_SCALE = SB_HEAD_DIM ** -0.5
MLA_SCALE = (MLA_NOPE_DIM + MLA_ROPE_DIM) ** -0.5
IN_WIDTH = 3 * SB_WIDTH + MLA_Q_RANK + MLA_KV_RANK + MLA_ROPE_DIM

LANES = 128
MXU_DIM = 256
IN_PAD = 2048
QM_SLOT = 256
NEG_BIG = -1e30

ROW_TILE = 512
FF_CHUNK = 256
ATT_BLOCK = 256
PAGES_PER_STEP = 8
SAMPLE_ROWS = HEADS * 4
ROWWISE_VMEM = 56 * 1024 * 1024
ATTN_VMEM = 48 * 1024 * 1024


def _const_spec(shape):
    zeros = (0,) * len(shape)
    return pl.BlockSpec(shape, lambda *_: zeros, pipeline_mode=pl.Buffered(1))


def _layer_norm(y, g, b):
    mu = jnp.mean(y, axis=-1, keepdims=True)
    d = y - mu
    var = jnp.mean(d * d, axis=-1, keepdims=True)
    return d * lax.rsqrt(var + LN_EPS) * g + b


def _rms_norm(y, g):
    return y * lax.rsqrt(jnp.mean(y * y, axis=-1, keepdims=True) + RMS_EPS) * g


def _dot(a, b):
    return jnp.dot(a, b, preferred_element_type=F32)


def _dot_nt(a, b):
    return lax.dot_general(a, b, (((1,), (1,)), ((), ())), preferred_element_type=F32)


def _swiglu(xb, wg_ref, wu_ref, wd_ref, h_ref):
    for c in range(0, D_FF, FF_CHUNK):
        g = _dot(xb, wg_ref[:, c:c + FF_CHUNK])
        u = _dot(xb, wu_ref[:, c:c + FF_CHUNK])
        h_ref[:, c:c + FF_CHUNK] = (g * jax.nn.sigmoid(g) * u).astype(BF16)
    return _dot(h_ref[...], wd_ref[...])


def _rope_lanes(x, c, s):
    lane = lax.broadcasted_iota(jnp.int32, x.shape, 1)
    swapped = jnp.where((lane & 31) < 16, pltpu.roll(x, LANES - 16, 1), pltpu.roll(x, 16, 1))
    return x * c + swapped * s


def _ffn_ln_kernel(x_ref, wg_ref, wu_ref, wd_ref, g_ref, b_ref, o_ref, h_ref):
    x = x_ref[...]
    f = _swiglu(x.astype(BF16), wg_ref, wu_ref, wd_ref, h_ref)
    o_ref[...] = _layer_norm(DEEPNORM_ALPHA * x + 0.5 * f, g_ref[...], b_ref[...])


def _ffn_ln(x, wg, wu, wd, g, b):
    rows = x.shape[0]
    tm = min(ROW_TILE, rows)
    return pl.pallas_call(
        _ffn_ln_kernel,
        grid=(rows // tm,),
        in_specs=[pl.BlockSpec((tm, D_MODEL), lambda i: (i, 0)),
                  _const_spec((D_MODEL, D_FF)), _const_spec((D_MODEL, D_FF)), _const_spec((D_FF, D_MODEL)),
                  _const_spec((1, D_MODEL)), _const_spec((1, D_MODEL))],
        out_specs=pl.BlockSpec((tm, D_MODEL), lambda i: (i, 0)),
        out_shape=jax.ShapeDtypeStruct((rows, D_MODEL), F32),
        scratch_shapes=[pltpu.VMEM((tm, D_FF), BF16)],
        compiler_params=pltpu.CompilerParams(dimension_semantics=("parallel",), vmem_limit_bytes=ROWWISE_VMEM),
        name="ffn_ln",
    )(x, wg, wu, wd, g, b)


def _proj_kernel(h_ref, win_ref, qg_ref, wuq_ref, kvg_ref, wuk_ref, psel_ref, cq_ref, sq_ref, ck_ref, sk_ref,
                 k_out, v_out, ckv_out, kr_out, qsb_out, ksb_out, vsb_out, qm_out, kcat_out):
    p = _dot(h_ref[...].astype(BF16), win_ref[...])
    sq = p[:, 0:SB_WIDTH]
    sk = p[:, SB_WIDTH:2 * SB_WIDTH]
    sv = p[:, 2 * SB_WIDTH:3 * SB_WIDTH]
    cq = p[:, 3 * SB_WIDTH:3 * SB_WIDTH + MLA_Q_RANK]
    ckv = p[:, 3 * SB_WIDTH + MLA_Q_RANK:IN_PAD - LANES]
    krp = p[:, IN_PAD - LANES:]
    k_out[...] = sk
    v_out[...] = sv
    qsb_out[...] = (sq * SB_SCALE).astype(BF16)
    ksb_out[...] = sk.astype(BF16)
    vsb_out[...] = sv.astype(BF16)

    q = _dot(_rms_norm(cq, qg_ref[...]).astype(BF16), wuq_ref[...])
    qlat = _dot(q[:, :SB_WIDTH].astype(BF16), wuk_ref[...]) * MLA_SCALE
    qr = q[:, SB_WIDTH:] * MLA_SCALE
    qro = jnp.concatenate(
        [_rope_lanes(qr[:, :LANES], cq_ref[:, :LANES], sq_ref[:, :LANES]),
         _rope_lanes(qr[:, LANES:], cq_ref[:, LANES:], sq_ref[:, LANES:])], axis=1)
    qrs = _dot(qro.astype(BF16), psel_ref[...])
    for h in range(HEADS):
        qm_out[:, h * QM_SLOT:h * QM_SLOT + LANES] = qlat[:, h * LANES:(h + 1) * LANES].astype(BF16)
        qm_out[:, h * QM_SLOT + LANES:(h + 1) * QM_SLOT] = qrs[:, h * LANES:(h + 1) * LANES].astype(BF16)

    ckvn = _rms_norm(ckv, kvg_ref[...])
    ckv_out[...] = ckvn
    kro = _rope_lanes(krp, ck_ref[...], sk_ref[...])
    kr_out[...] = kro[:, :MLA_ROPE_DIM]
    kcat_out[:, :LANES] = ckvn.astype(BF16)
    kcat_out[:, LANES:] = kro.astype(BF16)


def _proj(h, win, qg, wuq, kvg, wuk_bd, psel, cq, sq, ck, sk):
    rows = h.shape[0]
    tm = min(ROW_TILE, rows)
    row = lambda w: pl.BlockSpec((tm, w), lambda i: (i, 0))
    widths_dtypes = [(SB_WIDTH, F32), (SB_WIDTH, F32), (MLA_KV_RANK, F32), (MLA_ROPE_DIM, F32),
                     (SB_WIDTH, BF16), (SB_WIDTH, BF16), (SB_WIDTH, BF16),
                     (HEADS * QM_SLOT, BF16), (2 * LANES, BF16)]
    return pl.pallas_call(
        _proj_kernel,
        grid=(rows // tm,),
        in_specs=[row(D_MODEL), _const_spec((D_MODEL, IN_PAD)), _const_spec((1, MLA_Q_RANK)),
                  _const_spec((MLA_Q_RANK, HEADS * (MLA_NOPE_DIM + MLA_ROPE_DIM))), _const_spec((1, MLA_KV_RANK)),
                  _const_spec((SB_WIDTH, HEADS * LANES)), _const_spec((2 * LANES, HEADS * LANES)),
                  row(2 * LANES), row(2 * LANES), row(LANES), row(LANES)],
        out_specs=[row(w) for w, _ in widths_dtypes],
        out_shape=[jax.ShapeDtypeStruct((rows, w), d) for w, d in widths_dtypes],
        compiler_params=pltpu.CompilerParams(dimension_semantics=("parallel",), vmem_limit_bytes=ROWWISE_VMEM),
        name="mixer_proj",
    )(h, win, qg, wuq, kvg, wuk_bd, psel, cq, sq, ck, sk)


def _mix_ffn_kernel(h_ref, osb_ref, omla_ref, sbg_ref, mlag_ref, wo_sb_ref, wo_mla_ref, g2_ref, b2_ref,
                    wg_ref, wu_ref, wd_ref, g3_ref, b3_ref, o_ref, hs_ref):
    a = _rms_norm(osb_ref[...], sbg_ref[...]).astype(BF16)
    b = _rms_norm(omla_ref[...], mlag_ref[...]).astype(BF16)
    mixed = _dot(a, wo_sb_ref[...]) + _dot(b, wo_mla_ref[...])
    h2 = _layer_norm(DEEPNORM_ALPHA * h_ref[...] + mixed, g2_ref[...], b2_ref[...])
    f = _swiglu(h2.astype(BF16), wg_ref, wu_ref, wd_ref, hs_ref)
    o_ref[...] = _layer_norm(DEEPNORM_ALPHA * h2 + 0.5 * f, g3_ref[...], b3_ref[...])


def _mix_ffn(h, osb, omla, sbg, mlag, wo_sb, wo_mla, g2, b2, wg, wu, wd, g3, b3):
    rows = h.shape[0]
    tm = min(ROW_TILE, rows)
    row = lambda w: pl.BlockSpec((tm, w), lambda i: (i, 0))
    vec = lambda w: _const_spec((1, w))
    return pl.pallas_call(
        _mix_ffn_kernel,
        grid=(rows // tm,),
        in_specs=[row(D_MODEL), row(SB_WIDTH), row(MLA_WIDTH), vec(SB_WIDTH), vec(MLA_WIDTH),
                  _const_spec((SB_WIDTH, D_MODEL)), _const_spec((MLA_WIDTH, D_MODEL)), vec(D_MODEL), vec(D_MODEL),
                  _const_spec((D_MODEL, D_FF)), _const_spec((D_MODEL, D_FF)), _const_spec((D_FF, D_MODEL)),
                  vec(D_MODEL), vec(D_MODEL)],
        out_specs=row(D_MODEL),
        out_shape=jax.ShapeDtypeStruct((rows, D_MODEL), F32),
        scratch_shapes=[pltpu.VMEM((tm, D_FF), BF16)],
        compiler_params=pltpu.CompilerParams(dimension_semantics=("parallel",), vmem_limit_bytes=ROWWISE_VMEM),
        name="mix_ffn",
    )(h, osb, omla, sbg, mlag, wo_sb, wo_mla, g2, b2, wg, wu, wd, g3, b3)


def _later_key_matrix(n):
    r = lax.broadcasted_iota(jnp.int32, (n, n), 0)
    c = lax.broadcasted_iota(jnp.int32, (n, n), 1)
    return jnp.where(r > c, 1.0, 0.0).astype(BF16)


def _stick_breaking_block(z, later, carry, visible):
    t = jnp.log1p(jnp.exp(-jnp.abs(z)))
    log_beta = jnp.minimum(z, 0.0) - t
    log_keep = -jnp.maximum(z, 0.0) - t
    if visible is not None:
        log_keep = jnp.where(visible, log_keep, 0.0)
    hi = log_keep.astype(BF16)
    lo = (log_keep - hi.astype(F32)).astype(BF16)
    after = _dot(hi, later) + _dot(lo, later) + carry
    w = jnp.exp(log_beta + after)
    if visible is not None:
        w = jnp.where(visible, w, 0.0)
    return w, carry + jnp.sum(log_keep, axis=-1, keepdims=True)


def _softmax_block(s, values, m_old, l_old, acc_old):
    m_new = jnp.maximum(m_old, jnp.max(s, axis=-1, keepdims=True))
    alpha = jnp.exp(m_old - m_new)
    p = jnp.exp(s - m_new)
    l_new = alpha * l_old + jnp.sum(p, axis=-1, keepdims=True)
    acc_new = alpha * acc_old + _dot(p.astype(BF16), values)
    return m_new, l_new, acc_new


def _sb_prompt_kernel(q_ref, k_ref, v_ref, o_ref, acc_ref, car_ref, *, blk):
    qi = pl.program_id(2)
    q2 = q_ref[...]
    lane = lax.broadcasted_iota(jnp.int32, q2.shape, 1)
    zero = jnp.zeros_like(q2)
    q_heads = (jnp.where(lane < SB_HEAD_DIM, q2, zero), jnp.where(lane >= SB_HEAD_DIM, q2, zero))
    later = _later_key_matrix(blk)
    rowq = lax.broadcasted_iota(jnp.int32, (blk, blk), 0)
    colk = lax.broadcasted_iota(jnp.int32, (blk, blk), 1)
    acc_ref[...] = jnp.zeros_like(acc_ref)
    car_ref[...] = jnp.zeros_like(car_ref)

    def block(kj, visible):
        off = pl.multiple_of(kj * blk, blk)
        k2 = k_ref[pl.ds(off, blk), :]
        v2 = v_ref[pl.ds(off, blk), :]
        for h in range(2):
            z = _dot_nt(q_heads[h], k2)
            w, car = _stick_breaking_block(z, later, car_ref[h], visible)
            acc_ref[h] += _dot(w.astype(BF16), v2)
            car_ref[h] = car

    block(qi, colk < rowq)

    def body(i, c):
        block(qi - 1 - i, None)
        return c

    lax.fori_loop(0, qi, body, 0)
    out_lane = lax.broadcasted_iota(jnp.int32, (blk, LANES), 1)
    o_ref[...] = jnp.where(out_lane < SB_HEAD_DIM, acc_ref[0], acc_ref[1])


def _sb_prompt(q, k, v):
    nb, lp, _ = q.shape
    blk = ATT_BLOCK
    pairs = SB_WIDTH // LANES
    return pl.pallas_call(
        functools.partial(_sb_prompt_kernel, blk=blk),
        grid=(nb, pairs, lp // blk),
        in_specs=[pl.BlockSpec((None, blk, LANES), lambda b, hp, i: (b, i, hp)),
                  pl.BlockSpec((None, lp, LANES), lambda b, hp, i: (b, 0, hp)),
                  pl.BlockSpec((None, lp, LANES), lambda b, hp, i: (b, 0, hp))],
        out_specs=pl.BlockSpec((None, blk, LANES), lambda b, hp, i: (b, i, hp)),
        out_shape=jax.ShapeDtypeStruct((nb, lp, SB_WIDTH), F32),
        scratch_shapes=[pltpu.VMEM((2, blk, LANES), F32), pltpu.VMEM((2, blk, 1), F32)],
        compiler_params=pltpu.CompilerParams(dimension_semantics=("parallel", "parallel", "parallel"),
                                             vmem_limit_bytes=ATTN_VMEM),
        name="sb_prompt",
    )(q, k, v)


def _mla_prompt_kernel(q_ref, kc_ref, wuv_ref, o_ref, m_ref, l_ref, acc_ref, *, blk):
    qi = pl.program_id(1)
    rowq = lax.broadcasted_iota(jnp.int32, (blk, blk), 0)
    colk = lax.broadcasted_iota(jnp.int32, (blk, blk), 1)
    m_ref[...] = jnp.full_like(m_ref, NEG_BIG)
    l_ref[...] = jnp.zeros_like(l_ref)
    acc_ref[...] = jnp.zeros_like(acc_ref)

    def block(kj, visible):
        off = pl.multiple_of(kj * blk, blk)
        kc = kc_ref[pl.ds(off, blk), :]
        values = kc[:, :MLA_KV_RANK]
        for h in range(HEADS):
            s = _dot_nt(q_ref[:, h * QM_SLOT:(h + 1) * QM_SLOT], kc)
            if visible is not None:
                s = jnp.where(visible, s, NEG_BIG)
            m_ref[h], l_ref[h], acc_ref[h] = _softmax_block(s, values, m_ref[h], l_ref[h], acc_ref[h])

    block(qi, colk <= rowq)

    def body(i, c):
        block(qi - 1 - i, None)
        return c

    lax.fori_loop(0, qi, body, 0)
    out = jnp.zeros((blk, MLA_WIDTH), F32)
    for h in range(HEADS):
        o_lat = (acc_ref[h] / l_ref[h]).astype(BF16)
        out = out + _dot(o_lat, wuv_ref[h * MLA_KV_RANK:(h + 1) * MLA_KV_RANK, :])
    o_ref[...] = out


def _mla_prompt(qm, kcat, wuv_bd):
    nb, lp, _ = qm.shape
    blk = ATT_BLOCK
    return pl.pallas_call(
        functools.partial(_mla_prompt_kernel, blk=blk),
        grid=(nb, lp // blk),
        in_specs=[pl.BlockSpec((None, blk, HEADS * QM_SLOT), lambda b, i: (b, i, 0)),
                  pl.BlockSpec((None, lp, QM_SLOT), lambda b, i: (b, 0, 0)),
                  _const_spec((HEADS * MLA_KV_RANK, MLA_WIDTH))],
        out_specs=pl.BlockSpec((None, blk, MLA_WIDTH), lambda b, i: (b, i, 0)),
        out_shape=jax.ShapeDtypeStruct((nb, lp, MLA_WIDTH), F32),
        scratch_shapes=[pltpu.VMEM((HEADS, blk, 1), F32), pltpu.VMEM((HEADS, blk, 1), F32),
                        pltpu.VMEM((HEADS, blk, MLA_KV_RANK), F32)],
        compiler_params=pltpu.CompilerParams(dimension_semantics=("parallel", "parallel"),
                                             vmem_limit_bytes=ATTN_VMEM),
        name="mla_prompt",
    )(qm, kcat, wuv_bd)


def _decode_kernel(pt_ref, qbd_ref, kn_ref, vn_ref, qlat_ref, qrope_ref, cn_ref, rn_ref, wuv_ref, *rest, ppg):
    del pt_ref
    k_pages, v_pages = rest[0:ppg], rest[ppg:2 * ppg]
    c_pages, r_pages = rest[2 * ppg:3 * ppg], rest[3 * ppg:4 * ppg]
    osb_ref, omla_ref, acc_ref, car_ref, m_ref, l_ref, macc_ref = rest[4 * ppg:]
    step = pl.program_id(1)
    rows = SAMPLE_ROWS
    qbd = qbd_ref[...].astype(BF16)
    qlat = qlat_ref[...].astype(BF16)
    qrope = qrope_ref[...].astype(BF16)
    later = _later_key_matrix(PAGE_SIZE)

    def sb_update(z, values, values_transposed, visible):
        w, car = _stick_breaking_block(z, later, car_ref[...], visible)
        wb = w.astype(BF16)
        acc_ref[...] += _dot_nt(wb, values) if values_transposed else _dot(wb, values)
        car_ref[...] = car

    def mla_update(s, latent, visible):
        if visible is not None:
            s = jnp.where(visible, s, NEG_BIG)
        m_ref[...], l_ref[...], macc_ref[...] = _softmax_block(s, latent, m_ref[...], l_ref[...], macc_ref[...])

    @pl.when(step == 0)
    def _():
        acc_ref[...] = jnp.zeros_like(acc_ref)
        car_ref[...] = jnp.zeros_like(car_ref)
        m_ref[...] = jnp.full_like(m_ref, NEG_BIG)
        l_ref[...] = jnp.zeros_like(l_ref)
        macc_ref[...] = jnp.zeros_like(macc_ref)
        q_idx = lax.broadcasted_iota(jnp.int32, (rows, PAGE_SIZE), 0) >> 3
        col = lax.broadcasted_iota(jnp.int32, (rows, PAGE_SIZE), 1)
        pad = lambda ref: jnp.concatenate(
            [ref[...], jnp.zeros((PAGE_SIZE - ref.shape[0], ref.shape[1]), F32)], axis=0).astype(BF16)
        sb_update(_dot_nt(qbd, pad(kn_ref)), pad(vn_ref), False, col < q_idx)
        latent = pad(cn_ref)
        mla_update(_dot_nt(qlat, latent) + _dot_nt(qrope, pad(rn_ref)), latent, col <= q_idx)

    for p in reversed(range(ppg)):
        sb_update(_dot(qbd, k_pages[p][...].astype(BF16)), v_pages[p][...].astype(BF16), True, None)
        latent = c_pages[p][...].astype(BF16)
        mla_update(_dot_nt(qlat, latent) + _dot(qrope, r_pages[p][...].astype(BF16)), latent, None)

    @pl.when(step == pl.num_programs(1) - 1)
    def _():
        r = lax.broadcasted_iota(jnp.int32, (rows, SB_WIDTH), 0)
        c = lax.broadcasted_iota(jnp.int32, (rows, SB_WIDTH), 1)
        own = (c >> 6) == (r & 7)
        fold = lambda x: jnp.sum(jnp.where(own, x, 0.0).reshape(rows // HEADS, HEADS, SB_WIDTH), axis=1)
        osb_ref[...] = fold(acc_ref[...])
        o_lat = (macc_ref[...] / l_ref[...]).astype(BF16)
        omla_ref[...] = fold(_dot(o_lat, wuv_ref[...]))


def _decode(page_table, qbd, kn, vn, qlat, qrope, cn, rn, wuv_flat, cache_k, cache_v, cache_c, cache_r):
    nseq, npages = page_table.shape
    ppg = PAGES_PER_STEP
    steps = npages // ppg

    def page_spec(page_shape, p):
        def index(s, j, pt):
            return (pt[s, (steps - 1 - j) * ppg + p], 0, 0)
        return pl.BlockSpec((None,) + page_shape, index)

    per_seq = lambda r, w: pl.BlockSpec((None, r, w), lambda s, j, pt: (s, 0, 0))
    new_rows = kn.shape[1]
    in_specs = [per_seq(SAMPLE_ROWS, SB_WIDTH), per_seq(new_rows, SB_WIDTH), per_seq(new_rows, SB_WIDTH),
                per_seq(SAMPLE_ROWS, MLA_KV_RANK), per_seq(SAMPLE_ROWS, MLA_ROPE_DIM),
                per_seq(new_rows, MLA_KV_RANK), per_seq(new_rows, MLA_ROPE_DIM),
                pl.BlockSpec((MLA_KV_RANK, MLA_WIDTH), lambda s, j, pt: (0, 0))]
    pages = []
    for cache in (cache_k, cache_v, cache_c, cache_r):
        for p in range(ppg):
            in_specs.append(page_spec(cache.shape[1:], p))
            pages.append(cache)
    nq = SAMPLE_ROWS // HEADS
    out_spec = pl.BlockSpec((None, nq, SB_WIDTH), lambda s, j, pt: (s, 0, 0))
    return pl.pallas_call(
        functools.partial(_decode_kernel, ppg=ppg),
        grid_spec=pltpu.PrefetchScalarGridSpec(
            num_scalar_prefetch=1,
            grid=(nseq, steps),
            in_specs=in_specs,
            out_specs=[out_spec, out_spec],
            scratch_shapes=[pltpu.VMEM((SAMPLE_ROWS, SB_WIDTH), F32), pltpu.VMEM((SAMPLE_ROWS, 1), F32),
                            pltpu.VMEM((SAMPLE_ROWS, 1), F32), pltpu.VMEM((SAMPLE_ROWS, 1), F32),
                            pltpu.VMEM((SAMPLE_ROWS, MLA_KV_RANK), F32)]),
        out_shape=[jax.ShapeDtypeStruct((nseq, nq, SB_WIDTH), F32), jax.ShapeDtypeStruct((nseq, nq, MLA_WIDTH), F32)],
        compiler_params=pltpu.CompilerParams(dimension_semantics=("parallel", "arbitrary"),
                                             vmem_limit_bytes=ATTN_VMEM),
        name="paged_decode",
    )(page_table, qbd, kn, vn, qlat, qrope, cn, rn, wuv_flat, *pages)


def _rope_tables(pos):
    inv = ROPE_BASE ** (-jnp.arange(0, MLA_ROPE_DIM, 2, dtype=F32) / MLA_ROPE_DIM)
    ang = pos.astype(F32)[:, None] * inv[None, :]
    cos, sin = jnp.cos(ang), jnp.sin(ang)
    c32 = jnp.concatenate([cos, cos], axis=1)
    s32 = jnp.concatenate([-sin, sin], axis=1)
    pad = jnp.zeros((pos.shape[0], LANES - MLA_ROPE_DIM), F32)
    return (jnp.tile(c32, (1, HEADS)), jnp.tile(s32, (1, HEADS)),
            jnp.concatenate([c32, pad], axis=1), jnp.concatenate([s32, pad], axis=1))


def kernel(x_prompt, x_sample, cache_sb_k, cache_sb_v, cache_mla_ckv, cache_mla_krope, page_table, meta_tokens,
           ffn1_w_gu, ffn1_w_down, ln1_g, ln1_b, w_in, q_norm_g, w_uq, kv_norm_g, w_uk, w_uv, sb_out_g, mla_out_g,
           w_o, ln2_g, ln2_b, ffn2_w_gu, ffn2_w_down, ln3_g, ln3_b):
    assert w_in.shape[0] == DEPTH
    nb, seq, _ = x_prompt.shape
    nseq, dec_seq, _ = x_sample.shape
    ctx = seq + N_META
    lp = -(-ctx // ATT_BLOCK) * ATT_BLOCK
    past = page_table.shape[1] * PAGE_SIZE
    eye = jnp.eye(HEADS, dtype=F32)

    def ffn_weights(w_gu, w_down):
        return w_gu[0, :, :D_FF].astype(BF16), w_gu[0, :, D_FF:].astype(BF16), w_down[0].astype(BF16)

    ffn1 = ffn_weights(ffn1_w_gu, ffn1_w_down)
    ffn2 = ffn_weights(ffn2_w_gu, ffn2_w_down)
    win = jnp.pad(w_in[0], ((0, 0), (0, IN_PAD - IN_WIDTH))).astype(BF16)
    qk = MLA_NOPE_DIM + MLA_ROPE_DIM
    head_base = jnp.arange(HEADS)[:, None] * qk
    perm = jnp.concatenate([(head_base + jnp.arange(MLA_NOPE_DIM)[None, :]).reshape(-1),
                            (head_base + MLA_NOPE_DIM + jnp.arange(MLA_ROPE_DIM)[None, :]).reshape(-1)])
    wuq = w_uq[0][:, perm].astype(BF16)
    wuk_t = jnp.transpose(w_uk[0], (1, 2, 0))
    wuk_bd = (wuk_t[:, :, None, :] * eye[:, None, :, None]).reshape(SB_WIDTH, HEADS * MLA_KV_RANK).astype(BF16)
    wuv_t = jnp.transpose(w_uv[0], (1, 0, 2))
    wuv_bd = (wuv_t[:, :, None, :] * eye[:, None, :, None]).reshape(HEADS * MLA_KV_RANK, MLA_WIDTH).astype(BF16)
    wuv_flat = w_uv[0].reshape(MLA_KV_RANK, MLA_WIDTH).astype(BF16)
    sel_src = jnp.arange(HEADS * MLA_ROPE_DIM)
    sel_dst = (sel_src // MLA_ROPE_DIM) * LANES + sel_src % MLA_ROPE_DIM
    psel = jnp.zeros((HEADS * MLA_ROPE_DIM, HEADS * LANES), F32).at[sel_src, sel_dst].set(1.0).astype(BF16)
    wo_sb, wo_mla = w_o[0, :SB_WIDTH].astype(BF16), w_o[0, SB_WIDTH:].astype(BF16)

    meta = jnp.broadcast_to(meta_tokens[None].astype(x_prompt.dtype), (nb, N_META, D_MODEL))
    xp = jnp.concatenate([meta, x_prompt, jnp.zeros((nb, lp - ctx, D_MODEL), x_prompt.dtype)], axis=1)
    xp = xp.reshape(nb * lp, D_MODEL)
    xs = x_sample.reshape(nseq * dec_seq, D_MODEL)
    tab_p = _rope_tables(jnp.tile(jnp.arange(lp), nb))
    tab_s = _rope_tables(jnp.tile(past + jnp.arange(dec_seq), nseq))

    def front(x, tabs):
        h1 = _ffn_ln(x, *ffn1, ln1_g, ln1_b)
        return h1, _proj(h1, win, q_norm_g, wuq, kv_norm_g, wuk_bd, psel, *tabs)

    def back(h1, osb, omla):
        return _mix_ffn(h1, osb, omla, sb_out_g, mla_out_g, wo_sb, wo_mla, ln2_g, ln2_b, *ffn2, ln3_g, ln3_b)

    h1p, (kp, vp, cp, rp, qsb_p, ksb_p, vsb_p, qm_p, kcat_p) = front(xp, tab_p)
    b3 = lambda a: a.reshape(nb, lp, a.shape[-1])
    osb_p = _sb_prompt(b3(qsb_p), b3(ksb_p), b3(vsb_p))
    omla_p = _mla_prompt(b3(qm_p), b3(kcat_p), wuv_bd)
    yp = back(h1p, osb_p.reshape(nb * lp, SB_WIDTH), omla_p.reshape(nb * lp, MLA_WIDTH))

    h1s, (ks, vs, cs, rs, qsb_s, _, _, qm_s, _) = front(xs, tab_s)
    head_of_lane = jnp.arange(SB_WIDTH) // SB_HEAD_DIM
    head_mask = (head_of_lane[None, :] == jnp.arange(HEADS)[:, None]).astype(F32)
    qbd = (qsb_s.astype(F32).reshape(nseq, dec_seq, 1, SB_WIDTH) * head_mask[None, None]).reshape(
        nseq, dec_seq * HEADS, SB_WIDTH)
    qm4 = qm_s.astype(F32).reshape(nseq, dec_seq * HEADS, QM_SLOT)
    qlat_s = qm4[:, :, :MLA_KV_RANK]
    qrope_s = qm4[:, :, MLA_KV_RANK:MLA_KV_RANK + MLA_ROPE_DIM]
    new8 = lambda a: jnp.pad(a.reshape(nseq, dec_seq, a.shape[-1]), ((0, 0), (0, 8 - dec_seq), (0, 0)))
    feature_major = lambda c: jnp.moveaxis(c[0], 1, -1).reshape(c.shape[1], -1, PAGE_SIZE)
    osb_s, omla_s = _decode(
        page_table, qbd, new8(ks), new8(vs), qlat_s, qrope_s, new8(cs), new8(rs), wuv_flat,
        feature_major(cache_sb_k), feature_major(cache_sb_v), cache_mla_ckv[0], feature_major(cache_mla_krope))
    ys = back(h1s, osb_s.reshape(nseq * dec_seq, SB_WIDTH), omla_s.reshape(nseq * dec_seq, MLA_WIDTH))

    prompt = lambda a, shape: a.reshape((nb, lp) + shape)[:, :ctx][None]
    sample = lambda a, shape: a.reshape((nseq, dec_seq) + shape)[None]
    hd = (HEADS, SB_HEAD_DIM)
    return (yp.reshape(nb, lp, D_MODEL)[:, N_META:ctx], ys.reshape(nseq, dec_seq, D_MODEL),
            prompt(kp, hd), prompt(vp, hd), prompt(cp, (MLA_KV_RANK,)), prompt(rp, (MLA_ROPE_DIM,)),
            sample(ks, hd), sample(vs, hd), sample(cs, (MLA_KV_RANK,)), sample(rs, (MLA_ROPE_DIM,)))
```

```python
import functools

import jax
import jax.numpy as jnp
from jax import lax
from jax.experimental import pallas as pl
from jax.experimental.pallas import tpu as pltpu

F32 = jnp.float32
BF16 = jnp.bfloat16

D_MODEL = 1024
N_META = 16
HEADS = 8
SB_HEAD_DIM = 64
SB_WIDTH = HEADS * SB_HEAD_DIM
MLA_NOPE_DIM = 64
MLA_ROPE_DIM = 32
MLA_V_DIM = 64
MLA_WIDTH = HEADS * MLA_V_DIM
MLA_Q_RANK = 256
MLA_KV_RANK = 128
D_FF = 2816
PAGE_SIZE = 128
ROPE_BASE = 10000.0
LN_EPS = 1e-5
RMS_EPS = 1e-6
DEPTH = 1
DEEPNORM_ALPHA = (2 * DEPTH) ** 0.25
LOG2E = 1.4426950408889634
SB_QSCALE = SB_HEAD_DIM ** -0.5 * LOG2E
MLA_QSCALE = (MLA_NOPE_DIM + MLA_ROPE_DIM) ** -0.5 * LOG2E
IN_WIDTH = 3 * SB_WIDTH + MLA_Q_RANK + MLA_KV_RANK + MLA_ROPE_DIM

LANES = 128
MXU_DIM = 256
IN_PAD = 2048
QM_SLOT = 256
ONES_LANE = MLA_KV_RANK + MLA_ROPE_DIM
NEG_BIG = -1e30

ROW_TILE = 512
FF_CHUNK = 256
ATT_BLOCK = 256
SB_CHAIN_ROWS = 2048
PAGES_PER_STEP = 16
SAMPLE_ROWS = HEADS * 4
ROWWISE_VMEM = 56 * 1024 * 1024
ATTN_VMEM = 48 * 1024 * 1024


def _const_spec(shape):
    zeros = (0,) * len(shape)
    return pl.BlockSpec(shape, lambda *_: zeros, pipeline_mode=pl.Buffered(1))


def _layer_norm(y, g, b):
    mu = jnp.mean(y, axis=-1, keepdims=True)
    d = y - mu
    var = jnp.mean(d * d, axis=-1, keepdims=True)
    return d * lax.rsqrt(var + LN_EPS) * g + b


def _rms_norm(y, g):
    return y * lax.rsqrt(jnp.mean(y * y, axis=-1, keepdims=True) + RMS_EPS) * g


def _dot(a, b):
    return jnp.dot(a, b, preferred_element_type=F32)


def _dot_nt(a, b):
    return lax.dot_general(a, b, (((1,), (1,)), ((), ())), preferred_element_type=F32)


def _swiglu(xb, wg_ref, wu_ref, wd_ref, h_ref):
    for c in range(0, D_FF, FF_CHUNK):
        g = _dot(xb, wg_ref[:, c:c + FF_CHUNK])
        u = _dot(xb, wu_ref[:, c:c + FF_CHUNK])
        h_ref[:, c:c + FF_CHUNK] = (g * jax.nn.sigmoid(g) * u).astype(BF16)
    return _dot(h_ref[...], wd_ref[...])


def _rope_lanes(x, c, s):
    lane = lax.broadcasted_iota(jnp.int32, x.shape, 1)
    swapped = jnp.where((lane & 31) < 16, pltpu.roll(x, LANES - 16, 1), pltpu.roll(x, 16, 1))
    return x * c + swapped * s


def _ffn_ln_kernel(x_ref, wg_ref, wu_ref, wd_ref, g_ref, b_ref, o_ref, h_ref):
    x = x_ref[...]
    f = _swiglu(x.astype(BF16), wg_ref, wu_ref, wd_ref, h_ref)
    o_ref[...] = _layer_norm(DEEPNORM_ALPHA * x + 0.5 * f, g_ref[...], b_ref[...])


def _ffn_ln(x, wg, wu, wd, g, b):
    rows = x.shape[0]
    tm = min(ROW_TILE, rows)
    return pl.pallas_call(
        _ffn_ln_kernel,
        grid=(rows // tm,),
        in_specs=[pl.BlockSpec((tm, D_MODEL), lambda i: (i, 0)),
                  _const_spec((D_MODEL, D_FF)), _const_spec((D_MODEL, D_FF)), _const_spec((D_FF, D_MODEL)),
                  _const_spec((1, D_MODEL)), _const_spec((1, D_MODEL))],
        out_specs=pl.BlockSpec((tm, D_MODEL), lambda i: (i, 0)),
        out_shape=jax.ShapeDtypeStruct((rows, D_MODEL), F32),
        scratch_shapes=[pltpu.VMEM((tm, D_FF), BF16)],
        compiler_params=pltpu.CompilerParams(dimension_semantics=("parallel",), vmem_limit_bytes=ROWWISE_VMEM),
        name="ffn_ln",
    )(x, wg, wu, wd, g, b)


def _proj_kernel(h_ref, win_ref, qg_ref, wuq_ref, kvg_ref, wuk_ref, psel_ref, cq_ref, sq_ref, ck_ref, sk_ref,
                 k_out, v_out, ckv_out, kr_out, qsb_out, ksb_out, vsb_out, qm_out, kcat_out):
    p = _dot(h_ref[...].astype(BF16), win_ref[...])
    sq = p[:, 0:SB_WIDTH]
    sk = p[:, SB_WIDTH:2 * SB_WIDTH]
    sv = p[:, 2 * SB_WIDTH:3 * SB_WIDTH]
    cq = p[:, 3 * SB_WIDTH:3 * SB_WIDTH + MLA_Q_RANK]
    ckv = p[:, 3 * SB_WIDTH + MLA_Q_RANK:IN_PAD - LANES]
    krp = p[:, IN_PAD - LANES:]
    k_out[...] = sk
    v_out[...] = sv
    qsb_out[...] = (sq * SB_QSCALE).astype(BF16)
    ksb_out[...] = sk.astype(BF16)
    vsb_out[...] = sv.astype(BF16)

    q = _dot(_rms_norm(cq, qg_ref[...]).astype(BF16), wuq_ref[...])
    qlat = _dot(q[:, :SB_WIDTH].astype(BF16), wuk_ref[...]) * MLA_QSCALE
    qr = q[:, SB_WIDTH:] * MLA_QSCALE
    qro = jnp.concatenate(
        [_rope_lanes(qr[:, :LANES], cq_ref[:, :LANES], sq_ref[:, :LANES]),
         _rope_lanes(qr[:, LANES:], cq_ref[:, LANES:], sq_ref[:, LANES:])], axis=1)
    qrs = _dot(qro.astype(BF16), psel_ref[...])
    for h in range(HEADS):
        qm_out[:, h * QM_SLOT:h * QM_SLOT + LANES] = qlat[:, h * LANES:(h + 1) * LANES].astype(BF16)
        qm_out[:, h * QM_SLOT + LANES:(h + 1) * QM_SLOT] = qrs[:, h * LANES:(h + 1) * LANES].astype(BF16)

    ckvn = _rms_norm(ckv, kvg_ref[...])
    ckv_out[...] = ckvn
    kro = _rope_lanes(krp, ck_ref[...], sk_ref[...])
    kr_out[...] = kro[:, :MLA_ROPE_DIM]
    kcat_out[:, :LANES] = ckvn.astype(BF16)
    lane = lax.broadcasted_iota(jnp.int32, kro.shape, 1)
    kcat_out[:, LANES:] = jnp.where(lane == ONES_LANE - LANES, 1.0, kro).astype(BF16)


def _proj(h, win, qg, wuq, kvg, wuk_bd, psel, cq, sq, ck, sk):
    rows = h.shape[0]
    tm = min(ROW_TILE, rows)
    row = lambda w: pl.BlockSpec((tm, w), lambda i: (i, 0))
    widths_dtypes = [(SB_WIDTH, F32), (SB_WIDTH, F32), (MLA_KV_RANK, F32), (MLA_ROPE_DIM, F32),
                     (SB_WIDTH, BF16), (SB_WIDTH, BF16), (SB_WIDTH, BF16),
                     (HEADS * QM_SLOT, BF16), (2 * LANES, BF16)]
    return pl.pallas_call(
        _proj_kernel,
        grid=(rows // tm,),
        in_specs=[row(D_MODEL), _const_spec((D_MODEL, IN_PAD)), _const_spec((1, MLA_Q_RANK)),
                  _const_spec((MLA_Q_RANK, HEADS * (MLA_NOPE_DIM + MLA_ROPE_DIM))), _const_spec((1, MLA_KV_RANK)),
                  _const_spec((SB_WIDTH, HEADS * LANES)), _const_spec((2 * LANES, HEADS * LANES)),
                  row(2 * LANES), row(2 * LANES), row(LANES), row(LANES)],
        out_specs=[row(w) for w, _ in widths_dtypes],
        out_shape=[jax.ShapeDtypeStruct((rows, w), d) for w, d in widths_dtypes],
        compiler_params=pltpu.CompilerParams(dimension_semantics=("parallel",), vmem_limit_bytes=ROWWISE_VMEM),
        name="mixer_proj",
    )(h, win, qg, wuq, kvg, wuk_bd, psel, cq, sq, ck, sk)


def _mix_ffn_kernel(h_ref, osb_ref, omla_ref, sbg_ref, mlag_ref, wo_sb_ref, wo_mla_ref, g2_ref, b2_ref,
                    wg_ref, wu_ref, wd_ref, g3_ref, b3_ref, o_ref, hs_ref):
    a = _rms_norm(osb_ref[...], sbg_ref[...]).astype(BF16)
    b = _rms_norm(omla_ref[...], mlag_ref[...]).astype(BF16)
    mixed = _dot(a, wo_sb_ref[...]) + _dot(b, wo_mla_ref[...])
    h2 = _layer_norm(DEEPNORM_ALPHA * h_ref[...] + mixed, g2_ref[...], b2_ref[...])
    f = _swiglu(h2.astype(BF16), wg_ref, wu_ref, wd_ref, hs_ref)
    o_ref[...] = _layer_norm(DEEPNORM_ALPHA * h2 + 0.5 * f, g3_ref[...], b3_ref[...])


def _mix_ffn(h, osb, omla, sbg, mlag, wo_sb, wo_mla, g2, b2, wg, wu, wd, g3, b3):
    rows = h.shape[0]
    tm = min(ROW_TILE, rows)
    row = lambda w: pl.BlockSpec((tm, w), lambda i: (i, 0))
    vec = lambda w: _const_spec((1, w))
    return pl.pallas_call(
        _mix_ffn_kernel,
        grid=(rows // tm,),
        in_specs=[row(D_MODEL), row(SB_WIDTH), row(MLA_WIDTH), vec(SB_WIDTH), vec(MLA_WIDTH),
                  _const_spec((SB_WIDTH, D_MODEL)), _const_spec((MLA_WIDTH, D_MODEL)), vec(D_MODEL), vec(D_MODEL),
                  _const_spec((D_MODEL, D_FF)), _const_spec((D_MODEL, D_FF)), _const_spec((D_FF, D_MODEL)),
                  vec(D_MODEL), vec(D_MODEL)],
        out_specs=row(D_MODEL),
        out_shape=jax.ShapeDtypeStruct((rows, D_MODEL), F32),
        scratch_shapes=[pltpu.VMEM((tm, D_FF), BF16)],
        compiler_params=pltpu.CompilerParams(dimension_semantics=("parallel",), vmem_limit_bytes=ROWWISE_VMEM),
        name="mix_ffn",
    )(h, osb, omla, sbg, mlag, wo_sb, wo_mla, g2, b2, wg, wu, wd, g3, b3)


def _later_key_matrix(n):
    r = lax.broadcasted_iota(jnp.int32, (n, n), 0)
    c = lax.broadcasted_iota(jnp.int32, (n, n), 1)
    return jnp.where(r > c, 1.0, 0.0).astype(BF16)


def _stick_breaking_terms(z, visible):
    t = jnp.log(1.0 + jnp.exp2(-jnp.abs(z))) * LOG2E
    log_beta = jnp.minimum(z, 0.0) - t
    log_keep = log_beta - z
    if visible is not None:
        log_keep = jnp.where(visible, log_keep, 0.0)
    hi = log_keep.astype(BF16)
    lo = (log_keep - hi.astype(F32)).astype(BF16)
    return log_beta, log_keep, hi, lo


def _lane_tiled(x, width):
    return jnp.concatenate([x] * (width // LANES), axis=1) if width > LANES else x


def _sb_prompt_kernel(q_ref, k_ref, v_ref, o_ref, qh_ref, acc_ref, car_ref, *, blk):
    qi = pl.program_id(1)
    pairs = SB_WIDTH // LANES
    pair_rows = 2 * blk
    lane = lax.broadcasted_iota(jnp.int32, (blk, LANES), 1)
    for hp in range(pairs):
        q2 = q_ref[:, hp * LANES:(hp + 1) * LANES]
        qh_ref[pl.ds(hp * pair_rows, blk)] = jnp.where(lane < SB_HEAD_DIM, q2, jnp.zeros_like(q2))
        qh_ref[pl.ds(hp * pair_rows + blk, blk)] = jnp.where(lane >= SB_HEAD_DIM, q2, jnp.zeros_like(q2))
    later = _later_key_matrix(blk)
    later2 = jnp.concatenate([later, later], axis=0)
    rowq = lax.broadcasted_iota(jnp.int32, (blk, blk), 0)
    colk = lax.broadcasted_iota(jnp.int32, (blk, blk), 1)
    acc_ref[...] = jnp.zeros_like(acc_ref)
    car_ref[...] = jnp.zeros_like(car_ref)
    chain_pairs = SB_CHAIN_ROWS // pair_rows

    def block(kj, visible):
        off = pl.multiple_of(kj * blk, blk)
        if visible is not None:
            visible = jnp.concatenate([visible] * (SB_CHAIN_ROWS // blk), axis=0)
        for c in range(pairs // chain_pairs):
            chain = range(c * chain_pairs, (c + 1) * chain_pairs)
            rows = pl.ds(c * SB_CHAIN_ROWS, SB_CHAIN_ROWS)
            z = jnp.concatenate(
                [_dot_nt(qh_ref[pl.ds(hp * pair_rows, pair_rows)], k_ref[pl.ds(off, blk), hp * LANES:(hp + 1) * LANES])
                 for hp in chain], axis=0)
            log_beta, log_keep, hi, lo = _stick_breaking_terms(z, visible)
            carry = car_ref[rows]
            after = _dot(jnp.concatenate([hi, lo], axis=1), later2) + _lane_tiled(carry, blk)
            w = jnp.exp2(log_beta + after)
            if visible is not None:
                w = jnp.where(visible, w, 0.0)
            wb = w.astype(BF16)
            for n, hp in enumerate(chain):
                acc_ref[pl.ds(hp * pair_rows, pair_rows)] += _dot(
                    wb[n * pair_rows:(n + 1) * pair_rows], v_ref[pl.ds(off, blk), hp * LANES:(hp + 1) * LANES])
            car_ref[rows] = carry + jnp.sum(log_keep, axis=-1, keepdims=True)

    block(qi, colk < rowq)

    def body(i, c):
        block(qi - 1 - i, None)
        return c

    lax.fori_loop(0, qi, body, 0)
    for hp in range(pairs):
        o_ref[:, hp * LANES:(hp + 1) * LANES] = jnp.where(
            lane < SB_HEAD_DIM, acc_ref[pl.ds(hp * pair_rows, blk)], acc_ref[pl.ds(hp * pair_rows + blk, blk)])


def _sb_prompt(q, k, v):
    nb, lp, _ = q.shape
    blk = ATT_BLOCK
    resident = pl.BlockSpec((None, lp, SB_WIDTH), lambda b, i: (b, 0, 0), pipeline_mode=pl.Buffered(1))
    return pl.pallas_call(
        functools.partial(_sb_prompt_kernel, blk=blk),
        grid=(nb, lp // blk),
        in_specs=[pl.BlockSpec((None, blk, SB_WIDTH), lambda b, i: (b, i, 0)), resident, resident],
        out_specs=pl.BlockSpec((None, blk, SB_WIDTH), lambda b, i: (b, i, 0)),
        out_shape=jax.ShapeDtypeStruct((nb, lp, SB_WIDTH), F32),
        scratch_shapes=[pltpu.VMEM((HEADS * blk, LANES), BF16), pltpu.VMEM((HEADS * blk, LANES), F32),
                        pltpu.VMEM((HEADS * blk, LANES), F32)],
        compiler_params=pltpu.CompilerParams(dimension_semantics=("parallel", "parallel"),
                                             vmem_limit_bytes=ATTN_VMEM),
        name="sb_prompt",
    )(q, k, v)


def _mla_prompt_kernel(q_ref, kc_ref, wuv_ref, o_ref, m_ref, acc_ref, *, blk):
    qi = pl.program_id(1)
    rowq = lax.broadcasted_iota(jnp.int32, (blk, blk), 0)
    colk = lax.broadcasted_iota(jnp.int32, (blk, blk), 1)
    m_ref[...] = jnp.full_like(m_ref, NEG_BIG)
    acc_ref[...] = jnp.zeros_like(acc_ref)

    def block(off, nkeys, visible):
        kc = kc_ref[pl.ds(pl.multiple_of(off, blk), nkeys), :]
        for h in range(HEADS):
            s = _dot_nt(q_ref[:, h * QM_SLOT:(h + 1) * QM_SLOT], kc)
            if visible is not None:
                s = jnp.where(visible, s, NEG_BIG)
            m_old = m_ref[h]
            m_new = jnp.maximum(m_old, jnp.max(s, axis=-1, keepdims=True))
            p = jnp.exp2(s - _lane_tiled(m_new, nkeys))
            acc_ref[h] = _lane_tiled(jnp.exp2(m_old - m_new), QM_SLOT) * acc_ref[h] + _dot(p.astype(BF16), kc)
            m_ref[h] = m_new

    block(qi * blk, blk, colk <= rowq)

    def body(i, c):
        block(i * (2 * blk), 2 * blk, None)
        return c

    lax.fori_loop(0, lax.shift_right_logical(qi, 1), body, 0)

    @pl.when((qi & 1) == 1)
    def _():
        block((qi - 1) * blk, blk, None)

    out = jnp.zeros((blk, MLA_WIDTH), F32)
    for h in range(HEADS):
        a = acc_ref[h]
        o_lat = (a[:, :MLA_KV_RANK] / a[:, ONES_LANE:ONES_LANE + 1]).astype(BF16)
        out = out + _dot(o_lat, wuv_ref[h * MLA_KV_RANK:(h + 1) * MLA_KV_RANK, :])
    o_ref[...] = out


def _mla_prompt(qm, kcat, wuv_bd):
    nb, lp, _ = qm.shape
    blk = ATT_BLOCK
    return pl.pallas_call(
        functools.partial(_mla_prompt_kernel, blk=blk),
        grid=(nb, lp // blk),
        in_specs=[pl.BlockSpec((None, blk, HEADS * QM_SLOT), lambda b, i: (b, i, 0)),
                  pl.BlockSpec((None, lp, QM_SLOT), lambda b, i: (b, 0, 0)),
                  _const_spec((HEADS * MLA_KV_RANK, MLA_WIDTH))],
        out_specs=pl.BlockSpec((None, blk, MLA_WIDTH), lambda b, i: (b, i, 0)),
        out_shape=jax.ShapeDtypeStruct((nb, lp, MLA_WIDTH), F32),
        scratch_shapes=[pltpu.VMEM((HEADS, blk, LANES), F32), pltpu.VMEM((HEADS, blk, QM_SLOT), F32)],
        compiler_params=pltpu.CompilerParams(dimension_semantics=("parallel", "parallel"),
                                             vmem_limit_bytes=ATTN_VMEM),
        name="mla_prompt",
    )(qm, kcat, wuv_bd)


def _decode_kernel(pt_ref, qbd_ref, kn_ref, vn_ref, qlat_ref, qrope_ref, cn_ref, rn_ref, wuv_ref, *rest, ppg):
    del pt_ref
    k_pages, v_pages = rest[0:ppg], rest[ppg:2 * ppg]
    c_pages, r_pages = rest[2 * ppg:3 * ppg], rest[3 * ppg:4 * ppg]
    osb_ref, omla_ref, acc_ref, car_ref, m_ref, l_ref, macc_ref = rest[4 * ppg:]
    step = pl.program_id(1)
    rows = SAMPLE_ROWS
    qbd = qbd_ref[...].astype(BF16)
    qlat = qlat_ref[...].astype(BF16)
    qrope = qrope_ref[...].astype(BF16)
    later = _later_key_matrix(PAGE_SIZE)
    later2 = jnp.concatenate([later, later], axis=0)

    def sb_update(z, values, values_transposed, visible):
        n = z.shape[1] // PAGE_SIZE
        page = lambda x, p: x[:, p * PAGE_SIZE:(p + 1) * PAGE_SIZE]
        log_beta, log_keep, hi, lo = _stick_breaking_terms(z, visible)
        stacked = jnp.concatenate(
            [jnp.concatenate([page(hi, p), page(lo, p)], axis=1) for p in range(n)], axis=0)
        local = _dot(stacked, later2)
        carry = car_ref[...]
        after = [None] * n
        for p in reversed(range(n)):
            after[p] = local[p * rows:(p + 1) * rows] + carry
            carry = carry + jnp.sum(page(log_keep, p), axis=-1, keepdims=True)
        w = jnp.exp2(log_beta + jnp.concatenate(after, axis=1))
        if visible is not None:
            w = jnp.where(visible, w, 0.0)
        wb = w.astype(BF16)
        acc_ref[...] += _dot_nt(wb, values) if values_transposed else _dot(wb, values)
        car_ref[...] = carry

    def mla_update(s, latent, visible):
        if visible is not None:
            s = jnp.where(visible, s, NEG_BIG)
        m_old = m_ref[...]
        m_new = jnp.maximum(m_old, jnp.max(s, axis=-1, keepdims=True))
        alpha = jnp.exp2(m_old - m_new)
        p = jnp.exp2(s - m_new)
        l_ref[...] = alpha * l_ref[...] + jnp.sum(p, axis=-1, keepdims=True)
        macc_ref[...] = alpha * macc_ref[...] + _dot(p.astype(BF16), latent)
        m_ref[...] = m_new

    @pl.when(step == 0)
    def _():
        acc_ref[...] = jnp.zeros_like(acc_ref)
        car_ref[...] = jnp.zeros_like(car_ref)
        m_ref[...] = jnp.full_like(m_ref, NEG_BIG)
        l_ref[...] = jnp.zeros_like(l_ref)
        macc_ref[...] = jnp.zeros_like(macc_ref)
        q_idx = lax.broadcasted_iota(jnp.int32, (rows, PAGE_SIZE), 0) >> 3
        col = lax.broadcasted_iota(jnp.int32, (rows, PAGE_SIZE), 1)
        pad = lambda ref: jnp.concatenate(
            [ref[...], jnp.zeros((PAGE_SIZE - ref.shape[0], ref.shape[1]), F32)], axis=0).astype(BF16)
        sb_update(_dot_nt(qbd, pad(kn_ref)), pad(vn_ref), False, col < q_idx)
        latent = pad(cn_ref)
        mla_update(_dot_nt(qlat, latent) + _dot_nt(qrope, pad(rn_ref)), latent, col <= q_idx)

    gather = lambda refs, axis: jnp.concatenate([r[...].astype(BF16) for r in refs], axis=axis)
    sb_update(_dot(qbd, gather(k_pages, 1)), gather(v_pages, 1), True, None)
    latent = gather(c_pages, 0)
    mla_update(_dot_nt(qlat, latent) + _dot(qrope, gather(r_pages, 1)), latent, None)

    @pl.when(step == pl.num_programs(1) - 1)
    def _():
        r = lax.broadcasted_iota(jnp.int32, (rows, SB_WIDTH), 0)
        c = lax.broadcasted_iota(jnp.int32, (rows, SB_WIDTH), 1)
        own = (c >> 6) == (r & 7)
        fold = lambda x: jnp.sum(jnp.where(own, x, 0.0).reshape(rows // HEADS, HEADS, SB_WIDTH), axis=1)
        osb_ref[...] = fold(acc_ref[...])
        o_lat = (macc_ref[...] / l_ref[...]).astype(BF16)
        omla_ref[...] = fold(_dot(o_lat, wuv_ref[...]))


def _decode(page_table, qbd, kn, vn, qlat, qrope, cn, rn, wuv_flat, cache_k, cache_v, cache_c, cache_r):
    nseq, npages = page_table.shape
    ppg = PAGES_PER_STEP
    steps = npages // ppg

    def page_spec(page_shape, p):
        def index(s, j, pt):
            return (pt[s, (steps - 1 - j) * ppg + p], 0, 0)
        return pl.BlockSpec((None,) + page_shape, index)

    per_seq = lambda r, w: pl.BlockSpec((None, r, w), lambda s, j, pt: (s, 0, 0))
    new_rows = kn.shape[1]
    in_specs = [per_seq(SAMPLE_ROWS, SB_WIDTH), per_seq(new_rows, SB_WIDTH), per_seq(new_rows, SB_WIDTH),
                per_seq(SAMPLE_ROWS, MLA_KV_RANK), per_seq(SAMPLE_ROWS, MLA_ROPE_DIM),
                per_seq(new_rows, MLA_KV_RANK), per_seq(new_rows, MLA_ROPE_DIM),
                pl.BlockSpec((MLA_KV_RANK, MLA_WIDTH), lambda s, j, pt: (0, 0))]
    pages = []
    for cache in (cache_k, cache_v, cache_c, cache_r):
        for p in range(ppg):
            in_specs.append(page_spec(cache.shape[1:], p))
            pages.append(cache)
    nq = SAMPLE_ROWS // HEADS
    out_spec = pl.BlockSpec((None, nq, SB_WIDTH), lambda s, j, pt: (s, 0, 0))
    return pl.pallas_call(
        functools.partial(_decode_kernel, ppg=ppg),
        grid_spec=pltpu.PrefetchScalarGridSpec(
            num_scalar_prefetch=1,
            grid=(nseq, steps),
            in_specs=in_specs,
            out_specs=[out_spec, out_spec],
            scratch_shapes=[pltpu.VMEM((SAMPLE_ROWS, SB_WIDTH), F32), pltpu.VMEM((SAMPLE_ROWS, 1), F32),
                            pltpu.VMEM((SAMPLE_ROWS, 1), F32), pltpu.VMEM((SAMPLE_ROWS, 1), F32),
                            pltpu.VMEM((SAMPLE_ROWS, MLA_KV_RANK), F32)]),
        out_shape=[jax.ShapeDtypeStruct((nseq, nq, SB_WIDTH), F32), jax.ShapeDtypeStruct((nseq, nq, MLA_WIDTH), F32)],
        compiler_params=pltpu.CompilerParams(dimension_semantics=("parallel", "arbitrary"),
                                             vmem_limit_bytes=ATTN_VMEM),
        name="paged_decode",
    )(page_table, qbd, kn, vn, qlat, qrope, cn, rn, wuv_flat, *pages)


def _rope_tables(pos):
    inv = ROPE_BASE ** (-jnp.arange(0, MLA_ROPE_DIM, 2, dtype=F32) / MLA_ROPE_DIM)
    ang = pos.astype(F32)[:, None] * inv[None, :]
    cos, sin = jnp.cos(ang), jnp.sin(ang)
    c32 = jnp.concatenate([cos, cos], axis=1)
    s32 = jnp.concatenate([-sin, sin], axis=1)
    pad = jnp.zeros((pos.shape[0], LANES - MLA_ROPE_DIM), F32)
    return (jnp.tile(c32, (1, HEADS)), jnp.tile(s32, (1, HEADS)),
            jnp.concatenate([c32, pad], axis=1), jnp.concatenate([s32, pad], axis=1))


def kernel(x_prompt, x_sample, cache_sb_k, cache_sb_v, cache_mla_ckv, cache_mla_krope, page_table, meta_tokens,
           ffn1_w_gu, ffn1_w_down, ln1_g, ln1_b, w_in, q_norm_g, w_uq, kv_norm_g, w_uk, w_uv, sb_out_g, mla_out_g,
           w_o, ln2_g, ln2_b, ffn2_w_gu, ffn2_w_down, ln3_g, ln3_b):
    assert w_in.shape[0] == DEPTH
    nb, seq, _ = x_prompt.shape
    nseq, dec_seq, _ = x_sample.shape
    ctx = seq + N_META
    lp = -(-ctx // ATT_BLOCK) * ATT_BLOCK
    past = page_table.shape[1] * PAGE_SIZE
    eye = jnp.eye(HEADS, dtype=F32)

    def ffn_weights(w_gu, w_down):
        return w_gu[0, :, :D_FF].astype(BF16), w_gu[0, :, D_FF:].astype(BF16), w_down[0].astype(BF16)

    ffn1 = ffn_weights(ffn1_w_gu, ffn1_w_down)
    ffn2 = ffn_weights(ffn2_w_gu, ffn2_w_down)
    win = jnp.pad(w_in[0], ((0, 0), (0, IN_PAD - IN_WIDTH))).astype(BF16)
    qk = MLA_NOPE_DIM + MLA_ROPE_DIM
    head_base = jnp.arange(HEADS)[:, None] * qk
    perm = jnp.concatenate([(head_base + jnp.arange(MLA_NOPE_DIM)[None, :]).reshape(-1),
                            (head_base + MLA_NOPE_DIM + jnp.arange(MLA_ROPE_DIM)[None, :]).reshape(-1)])
    wuq = w_uq[0][:, perm].astype(BF16)
    wuk_t = jnp.transpose(w_uk[0], (1, 2, 0))
    wuk_bd = (wuk_t[:, :, None, :] * eye[:, None, :, None]).reshape(SB_WIDTH, HEADS * MLA_KV_RANK).astype(BF16)
    wuv_t = jnp.transpose(w_uv[0], (1, 0, 2))
    wuv_bd = (wuv_t[:, :, None, :] * eye[:, None, :, None]).reshape(HEADS * MLA_KV_RANK, MLA_WIDTH).astype(BF16)
    wuv_flat = w_uv[0].reshape(MLA_KV_RANK, MLA_WIDTH).astype(BF16)
    sel_src = jnp.arange(HEADS * MLA_ROPE_DIM)
    sel_dst = (sel_src // MLA_ROPE_DIM) * LANES + sel_src % MLA_ROPE_DIM
    psel = jnp.zeros((HEADS * MLA_ROPE_DIM, HEADS * LANES), F32).at[sel_src, sel_dst].set(1.0).astype(BF16)
    wo_sb, wo_mla = w_o[0, :SB_WIDTH].astype(BF16), w_o[0, SB_WIDTH:].astype(BF16)

    meta = jnp.broadcast_to(meta_tokens[None].astype(x_prompt.dtype), (nb, N_META, D_MODEL))
    xp = jnp.concatenate([meta, x_prompt, jnp.zeros((nb, lp - ctx, D_MODEL), x_prompt.dtype)], axis=1)
    xp = xp.reshape(nb * lp, D_MODEL)
    xs = x_sample.reshape(nseq * dec_seq, D_MODEL)
    tab_p = _rope_tables(jnp.tile(jnp.arange(lp), nb))
    tab_s = _rope_tables(jnp.tile(past + jnp.arange(dec_seq), nseq))

    def front(x, tabs):
        h1 = _ffn_ln(x, *ffn1, ln1_g, ln1_b)
        return h1, _proj(h1, win, q_norm_g, wuq, kv_norm_g, wuk_bd, psel, *tabs)

    def back(h1, osb, omla):
        return _mix_ffn(h1, osb, omla, sb_out_g, mla_out_g, wo_sb, wo_mla, ln2_g, ln2_b, *ffn2, ln3_g, ln3_b)

    h1p, (kp, vp, cp, rp, qsb_p, ksb_p, vsb_p, qm_p, kcat_p) = front(xp, tab_p)
    b3 = lambda a: a.reshape(nb, lp, a.shape[-1])
    osb_p = _sb_prompt(b3(qsb_p), b3(ksb_p), b3(vsb_p))
    omla_p = _mla_prompt(b3(qm_p), b3(kcat_p), wuv_bd)
    yp = back(h1p, osb_p.reshape(nb * lp, SB_WIDTH), omla_p.reshape(nb * lp, MLA_WIDTH))

    h1s, (ks, vs, cs, rs, qsb_s, _, _, qm_s, _) = front(xs, tab_s)
    head_of_lane = jnp.arange(SB_WIDTH) // SB_HEAD_DIM
    head_mask = (head_of_lane[None, :] == jnp.arange(HEADS)[:, None]).astype(F32)
    qbd = (qsb_s.astype(F32).reshape(nseq, dec_seq, 1, SB_WIDTH) * head_mask[None, None]).reshape(
        nseq, dec_seq * HEADS, SB_WIDTH)
    qm4 = qm_s.astype(F32).reshape(nseq, dec_seq * HEADS, QM_SLOT)
    qlat_s = qm4[:, :, :MLA_KV_RANK]
    qrope_s = qm4[:, :, MLA_KV_RANK:MLA_KV_RANK + MLA_ROPE_DIM]
    new8 = lambda a: jnp.pad(a.reshape(nseq, dec_seq, a.shape[-1]), ((0, 0), (0, 8 - dec_seq), (0, 0)))
    feature_major = lambda c: jnp.moveaxis(c[0], 1, -1).reshape(c.shape[1], -1, PAGE_SIZE)
    osb_s, omla_s = _decode(
        page_table, qbd, new8(ks), new8(vs), qlat_s, qrope_s, new8(cs), new8(rs), wuv_flat,
        feature_major(cache_sb_k), feature_major(cache_sb_v), cache_mla_ckv[0], feature_major(cache_mla_krope))
    ys = back(h1s, osb_s.reshape(nseq * dec_seq, SB_WIDTH), omla_s.reshape(nseq * dec_seq, MLA_WIDTH))

    prompt = lambda a, shape: a.reshape((nb, lp) + shape)[:, :ctx][None]
    sample = lambda a, shape: a.reshape((nseq, dec_seq) + shape)[None]
    hd = (HEADS, SB_HEAD_DIM)
    return (yp.reshape(nb, lp, D_MODEL)[:, N_META:ctx], ys.reshape(nseq, dec_seq, D_MODEL),
            prompt(kp, hd), prompt(vp, hd), prompt(cp, (MLA_KV_RANK,)), prompt(rp, (MLA_ROPE_DIM,)),
            sample(ks, hd), sample(vs, hd), sample(cs, (MLA_KV_RANK,)), sample(rs, (MLA_ROPE_DIM,)))
```

```python
import functools

import jax
import jax.numpy as jnp
from jax import lax
from jax.experimental import pallas as pl
from jax.experimental.pallas import tpu as pltpu

F32 = jnp.float32
BF16 = jnp.bfloat16

D_MODEL = 1024
N_META = 16
HEADS = 8
SB_HEAD_DIM = 64
SB_WIDTH = HEADS * SB_HEAD_DIM
MLA_NOPE_DIM = 64
MLA_ROPE_DIM = 32
MLA_V_DIM = 64
MLA_WIDTH = HEADS * MLA_V_DIM
MLA_Q_RANK = 256
MLA_KV_RANK = 128
D_FF = 2816
PAGE_SIZE = 128
ROPE_BASE = 10000.0
LN_EPS = 1e-5
RMS_EPS = 1e-6
DEPTH = 1
DEEPNORM_ALPHA = (2 * DEPTH) ** 0.25
LOG2E = 1.4426950408889634
SB_QSCALE = SB_HEAD_DIM ** -0.5 * LOG2E
MLA_QSCALE = (MLA_NOPE_DIM + MLA_ROPE_DIM) ** -0.5 * LOG2E
IN_WIDTH = 3 * SB_WIDTH + MLA_Q_RANK + MLA_KV_RANK + MLA_ROPE_DIM

LANES = 128
MXU_DIM = 256
IN_PAD = 2048
QM_SLOT = 256
ONES_LANE = MLA_KV_RANK + MLA_ROPE_DIM
NEG_BIG = -1e30

ROW_TILE = 512
FF_CHUNK = 256
ATT_BLOCK = 256
SB_CHAIN_ROWS = 2048
PAGES_PER_STEP = 16
SAMPLE_ROWS = HEADS * 4
ROWWISE_VMEM = 56 * 1024 * 1024
ATTN_VMEM = 48 * 1024 * 1024


def _const_spec(shape):
    zeros = (0,) * len(shape)
    return pl.BlockSpec(shape, lambda *_: zeros, pipeline_mode=pl.Buffered(1))


def _layer_norm(y, g, b):
    mu = jnp.mean(y, axis=-1, keepdims=True)
    d = y - mu
    var = jnp.mean(d * d, axis=-1, keepdims=True)
    return d * lax.rsqrt(var + LN_EPS) * g + b


def _rms_norm(y, g):
    return y * lax.rsqrt(jnp.mean(y * y, axis=-1, keepdims=True) + RMS_EPS) * g


def _dot(a, b):
    return jnp.dot(a, b, preferred_element_type=F32)


def _dot_nt(a, b):
    return lax.dot_general(a, b, (((1,), (1,)), ((), ())), preferred_element_type=F32)


def _swiglu(xb, wg_ref, wu_ref, wd_ref, h_ref):
    for c in range(0, D_FF, FF_CHUNK):
        g = _dot(xb, wg_ref[:, c:c + FF_CHUNK])
        u = _dot(xb, wu_ref[:, c:c + FF_CHUNK])
        h_ref[:, c:c + FF_CHUNK] = (g * jax.nn.sigmoid(g) * u).astype(BF16)
    return _dot(h_ref[...], wd_ref[...])


def _rope_lanes(x, c, s):
    lane = lax.broadcasted_iota(jnp.int32, x.shape, 1)
    swapped = jnp.where((lane & 31) < 16, pltpu.roll(x, LANES - 16, 1), pltpu.roll(x, 16, 1))
    return x * c + swapped * s


def _ffn_ln_kernel(x_ref, wg_ref, wu_ref, wd_ref, g_ref, b_ref, o_ref, h_ref):
    x = x_ref[...]
    f = _swiglu(x.astype(BF16), wg_ref, wu_ref, wd_ref, h_ref)
    o_ref[...] = _layer_norm(DEEPNORM_ALPHA * x + 0.5 * f, g_ref[...], b_ref[...])


def _ffn_ln(x, wg, wu, wd, g, b):
    rows = x.shape[0]
    tm = min(ROW_TILE, rows)
    return pl.pallas_call(
        _ffn_ln_kernel,
        grid=(rows // tm,),
        in_specs=[pl.BlockSpec((tm, D_MODEL), lambda i: (i, 0)),
                  _const_spec((D_MODEL, D_FF)), _const_spec((D_MODEL, D_FF)), _const_spec((D_FF, D_MODEL)),
                  _const_spec((1, D_MODEL)), _const_spec((1, D_MODEL))],
        out_specs=pl.BlockSpec((tm, D_MODEL), lambda i: (i, 0)),
        out_shape=jax.ShapeDtypeStruct((rows, D_MODEL), F32),
        scratch_shapes=[pltpu.VMEM((tm, D_FF), BF16)],
        compiler_params=pltpu.CompilerParams(dimension_semantics=("parallel",), vmem_limit_bytes=ROWWISE_VMEM),
        name="ffn_ln",
    )(x, wg, wu, wd, g, b)


def _proj_kernel(h_ref, win_ref, qg_ref, wuq_ref, kvg_ref, wuk_ref, psel_ref, cq_ref, sq_ref, ck_ref, sk_ref,
                 k_out, v_out, ckv_out, kr_out, qsb_out, ksb_out, vsb_out, qm_out, kcat_out, *, kv_position_minor):
    p = _dot(h_ref[...].astype(BF16), win_ref[...])
    sq = p[:, 0:SB_WIDTH]
    sk = p[:, SB_WIDTH:2 * SB_WIDTH]
    sv = p[:, 2 * SB_WIDTH:3 * SB_WIDTH]
    cq = p[:, 3 * SB_WIDTH:3 * SB_WIDTH + MLA_Q_RANK]
    ckv = p[:, 3 * SB_WIDTH + MLA_Q_RANK:IN_PAD - LANES]
    krp = p[:, IN_PAD - LANES:]
    k_out[...] = sk.T if kv_position_minor else sk
    v_out[...] = sv.T if kv_position_minor else sv
    qsb_out[...] = (sq * SB_QSCALE).astype(BF16)
    ksb_out[...] = sk.astype(BF16)
    vsb_out[...] = sv.astype(BF16)

    q = _dot(_rms_norm(cq, qg_ref[...]).astype(BF16), wuq_ref[...])
    qlat = _dot(q[:, :SB_WIDTH].astype(BF16), wuk_ref[...]) * MLA_QSCALE
    qr = q[:, SB_WIDTH:] * MLA_QSCALE
    qro = jnp.concatenate(
        [_rope_lanes(qr[:, :LANES], cq_ref[:, :LANES], sq_ref[:, :LANES]),
         _rope_lanes(qr[:, LANES:], cq_ref[:, LANES:], sq_ref[:, LANES:])], axis=1)
    qrs = _dot(qro.astype(BF16), psel_ref[...])
    for h in range(HEADS):
        qm_out[:, h * QM_SLOT:h * QM_SLOT + LANES] = qlat[:, h * LANES:(h + 1) * LANES].astype(BF16)
        qm_out[:, h * QM_SLOT + LANES:(h + 1) * QM_SLOT] = qrs[:, h * LANES:(h + 1) * LANES].astype(BF16)

    ckvn = _rms_norm(ckv, kvg_ref[...])
    ckv_out[...] = ckvn
    kro = _rope_lanes(krp, ck_ref[...], sk_ref[...])
    kr_out[...] = kro[:, :MLA_ROPE_DIM]
    kcat_out[:, :LANES] = ckvn.astype(BF16)
    lane = lax.broadcasted_iota(jnp.int32, kro.shape, 1)
    kcat_out[:, LANES:] = jnp.where(lane == ONES_LANE - LANES, 1.0, kro).astype(BF16)


def _proj(h, nb, kv_positions, win, qg, wuq, kvg, wuk_bd, psel, cq, sq, ck, sk):
    rows = h.shape[0]
    tm = min(ATT_BLOCK, rows)
    nblk = rows // nb // tm
    row = lambda w: pl.BlockSpec((tm, w), lambda b, i: (b * nblk + i, 0))
    tab = lambda w: pl.BlockSpec((tm, w), lambda b, i: (i, 0))
    widths_dtypes = [(MLA_KV_RANK, F32), (MLA_ROPE_DIM, F32), (SB_WIDTH, BF16), (SB_WIDTH, BF16), (SB_WIDTH, BF16),
                     (HEADS * QM_SLOT, BF16), (2 * LANES, BF16)]
    if kv_positions is None:
        kv_spec, kv_shape = row(SB_WIDTH), jax.ShapeDtypeStruct((rows, SB_WIDTH), F32)
    else:
        kv_spec = pl.BlockSpec((None, SB_WIDTH, tm), lambda b, i: (b, 0, i))
        kv_shape = jax.ShapeDtypeStruct((nb, SB_WIDTH, kv_positions), F32)
    return pl.pallas_call(
        functools.partial(_proj_kernel, kv_position_minor=kv_positions is not None),
        grid=(nb, nblk),
        in_specs=[row(D_MODEL), _const_spec((D_MODEL, IN_PAD)), _const_spec((1, MLA_Q_RANK)),
                  _const_spec((MLA_Q_RANK, HEADS * (MLA_NOPE_DIM + MLA_ROPE_DIM))), _const_spec((1, MLA_KV_RANK)),
                  _const_spec((SB_WIDTH, HEADS * LANES)), _const_spec((2 * LANES, HEADS * LANES)),
                  tab(2 * LANES), tab(2 * LANES), tab(LANES), tab(LANES)],
        out_specs=[kv_spec, kv_spec] + [row(w) for w, _ in widths_dtypes],
        out_shape=[kv_shape, kv_shape] + [jax.ShapeDtypeStruct((rows, w), d) for w, d in widths_dtypes],
        compiler_params=pltpu.CompilerParams(dimension_semantics=("parallel", "parallel"),
                                             vmem_limit_bytes=ROWWISE_VMEM),
        name="mixer_proj",
    )(h, win, qg, wuq, kvg, wuk_bd, psel, cq, sq, ck, sk)


def _mix_ffn_kernel(h_ref, osb_ref, omla_ref, sbg_ref, mlag_ref, wo_sb_ref, wo_mla_ref, g2_ref, b2_ref,
                    wg_ref, wu_ref, wd_ref, g3_ref, b3_ref, o_ref, hs_ref):
    a = _rms_norm(osb_ref[...], sbg_ref[...]).astype(BF16)
    b = _rms_norm(omla_ref[...], mlag_ref[...]).astype(BF16)
    mixed = _dot(a, wo_sb_ref[...]) + _dot(b, wo_mla_ref[...])
    h2 = _layer_norm(DEEPNORM_ALPHA * h_ref[...] + mixed, g2_ref[...], b2_ref[...])
    f = _swiglu(h2.astype(BF16), wg_ref, wu_ref, wd_ref, hs_ref)
    o_ref[...] = _layer_norm(DEEPNORM_ALPHA * h2 + 0.5 * f, g3_ref[...], b3_ref[...])


def _mix_ffn(h, osb, omla, sbg, mlag, wo_sb, wo_mla, g2, b2, wg, wu, wd, g3, b3):
    rows = h.shape[0]
    tm = min(ROW_TILE, rows)
    row = lambda w: pl.BlockSpec((tm, w), lambda i: (i, 0))
    vec = lambda w: _const_spec((1, w))
    return pl.pallas_call(
        _mix_ffn_kernel,
        grid=(rows // tm,),
        in_specs=[row(D_MODEL), row(SB_WIDTH), row(MLA_WIDTH), vec(SB_WIDTH), vec(MLA_WIDTH),
                  _const_spec((SB_WIDTH, D_MODEL)), _const_spec((MLA_WIDTH, D_MODEL)), vec(D_MODEL), vec(D_MODEL),
                  _const_spec((D_MODEL, D_FF)), _const_spec((D_MODEL, D_FF)), _const_spec((D_FF, D_MODEL)),
                  vec(D_MODEL), vec(D_MODEL)],
        out_specs=row(D_MODEL),
        out_shape=jax.ShapeDtypeStruct((rows, D_MODEL), F32),
        scratch_shapes=[pltpu.VMEM((tm, D_FF), BF16)],
        compiler_params=pltpu.CompilerParams(dimension_semantics=("parallel",), vmem_limit_bytes=ROWWISE_VMEM),
        name="mix_ffn",
    )(h, osb, omla, sbg, mlag, wo_sb, wo_mla, g2, b2, wg, wu, wd, g3, b3)


def _later_key_matrix(n):
    r = lax.broadcasted_iota(jnp.int32, (n, n), 0)
    c = lax.broadcasted_iota(jnp.int32, (n, n), 1)
    return jnp.where(r > c, 1.0, 0.0).astype(BF16)


def _stick_breaking_terms(z, visible):
    t = jnp.log(1.0 + jnp.exp2(-jnp.abs(z))) * LOG2E
    log_beta = jnp.minimum(z, 0.0) - t
    log_keep = log_beta - z
    if visible is not None:
        log_keep = jnp.where(visible, log_keep, 0.0)
    hi = log_keep.astype(BF16)
    lo = (log_keep - hi.astype(F32)).astype(BF16)
    return log_beta, log_keep, hi, lo


def _lane_tiled(x, width):
    return jnp.concatenate([x] * (width // LANES), axis=1) if width > LANES else x


def _sb_prompt_kernel(q_ref, k_ref, v_ref, o_ref, qh_ref, acc_ref, car_ref, *, blk):
    qi = pl.program_id(1)
    pairs = SB_WIDTH // LANES
    pair_rows = 2 * blk
    lane = lax.broadcasted_iota(jnp.int32, (blk, LANES), 1)
    for hp in range(pairs):
        q2 = q_ref[:, hp * LANES:(hp + 1) * LANES]
        qh_ref[pl.ds(hp * pair_rows, blk)] = jnp.where(lane < SB_HEAD_DIM, q2, jnp.zeros_like(q2))
        qh_ref[pl.ds(hp * pair_rows + blk, blk)] = jnp.where(lane >= SB_HEAD_DIM, q2, jnp.zeros_like(q2))
    later = _later_key_matrix(blk)
    later2 = jnp.concatenate([later, later], axis=0)
    rowq = lax.broadcasted_iota(jnp.int32, (blk, blk), 0)
    colk = lax.broadcasted_iota(jnp.int32, (blk, blk), 1)
    acc_ref[...] = jnp.zeros_like(acc_ref)
    car_ref[...] = jnp.zeros_like(car_ref)
    chain_pairs = SB_CHAIN_ROWS // pair_rows

    def block(kj, visible):
        off = pl.multiple_of(kj * blk, blk)
        if visible is not None:
            visible = jnp.concatenate([visible] * (SB_CHAIN_ROWS // blk), axis=0)
        for c in range(pairs // chain_pairs):
            chain = range(c * chain_pairs, (c + 1) * chain_pairs)
            rows = pl.ds(c * SB_CHAIN_ROWS, SB_CHAIN_ROWS)
            z = jnp.concatenate(
                [_dot_nt(qh_ref[pl.ds(hp * pair_rows, pair_rows)], k_ref[pl.ds(off, blk), hp * LANES:(hp + 1) * LANES])
                 for hp in chain], axis=0)
            log_beta, log_keep, hi, lo = _stick_breaking_terms(z, visible)
            carry = car_ref[rows]
            after = _dot(jnp.concatenate([hi, lo], axis=1), later2) + _lane_tiled(carry, blk)
            w = jnp.exp2(log_beta + after)
            if visible is not None:
                w = jnp.where(visible, w, 0.0)
            wb = w.astype(BF16)
            for n, hp in enumerate(chain):
                acc_ref[pl.ds(hp * pair_rows, pair_rows)] += _dot(
                    wb[n * pair_rows:(n + 1) * pair_rows], v_ref[pl.ds(off, blk), hp * LANES:(hp + 1) * LANES])
            car_ref[rows] = carry + jnp.sum(log_keep, axis=-1, keepdims=True)

    block(qi, colk < rowq)

    def body(i, c):
        block(qi - 1 - i, None)
        return c

    lax.fori_loop(0, qi, body, 0)
    for hp in range(pairs):
        o_ref[:, hp * LANES:(hp + 1) * LANES] = jnp.where(
            lane < SB_HEAD_DIM, acc_ref[pl.ds(hp * pair_rows, blk)], acc_ref[pl.ds(hp * pair_rows + blk, blk)])


def _sb_prompt(q, k, v):
    nb, lp, _ = q.shape
    blk = ATT_BLOCK
    resident = pl.BlockSpec((None, lp, SB_WIDTH), lambda b, i: (b, 0, 0), pipeline_mode=pl.Buffered(1))
    return pl.pallas_call(
        functools.partial(_sb_prompt_kernel, blk=blk),
        grid=(nb, lp // blk),
        in_specs=[pl.BlockSpec((None, blk, SB_WIDTH), lambda b, i: (b, i, 0)), resident, resident],
        out_specs=pl.BlockSpec((None, blk, SB_WIDTH), lambda b, i: (b, i, 0)),
        out_shape=jax.ShapeDtypeStruct((nb, lp, SB_WIDTH), F32),
        scratch_shapes=[pltpu.VMEM((HEADS * blk, LANES), BF16), pltpu.VMEM((HEADS * blk, LANES), F32),
                        pltpu.VMEM((HEADS * blk, LANES), F32)],
        compiler_params=pltpu.CompilerParams(dimension_semantics=("parallel", "parallel"),
                                             vmem_limit_bytes=ATTN_VMEM),
        name="sb_prompt",
    )(q, k, v)


def _mla_prompt_kernel(q_ref, kc_ref, wuv_ref, o_ref, m_ref, acc_ref, *, blk):
    qi = pl.program_id(1)
    rowq = lax.broadcasted_iota(jnp.int32, (blk, blk), 0)
    colk = lax.broadcasted_iota(jnp.int32, (blk, blk), 1)
    m_ref[...] = jnp.full_like(m_ref, NEG_BIG)
    acc_ref[...] = jnp.zeros_like(acc_ref)

    def block(off, nkeys, visible):
        kc = kc_ref[pl.ds(pl.multiple_of(off, blk), nkeys), :]
        for h in range(HEADS):
            s = _dot_nt(q_ref[:, h * QM_SLOT:(h + 1) * QM_SLOT], kc)
            if visible is not None:
                s = jnp.where(visible, s, NEG_BIG)
            m_old = m_ref[h]
            m_new = jnp.maximum(m_old, jnp.max(s, axis=-1, keepdims=True))
            p = jnp.exp2(s - _lane_tiled(m_new, nkeys))
            acc_ref[h] = _lane_tiled(jnp.exp2(m_old - m_new), QM_SLOT) * acc_ref[h] + _dot(p.astype(BF16), kc)
            m_ref[h] = m_new

    block(qi * blk, blk, colk <= rowq)

    def body(i, c):
        block(i * (2 * blk), 2 * blk, None)
        return c

    lax.fori_loop(0, lax.shift_right_logical(qi, 1), body, 0)

    @pl.when((qi & 1) == 1)
    def _():
        block((qi - 1) * blk, blk, None)

    out = jnp.zeros((blk, MLA_WIDTH), F32)
    for h in range(HEADS):
        a = acc_ref[h]
        o_lat = (a[:, :MLA_KV_RANK] / a[:, ONES_LANE:ONES_LANE + 1]).astype(BF16)
        out = out + _dot(o_lat, wuv_ref[h * MLA_KV_RANK:(h + 1) * MLA_KV_RANK, :])
    o_ref[...] = out


def _mla_prompt(qm, kcat, wuv_bd):
    nb, lp, _ = qm.shape
    blk = ATT_BLOCK
    return pl.pallas_call(
        functools.partial(_mla_prompt_kernel, blk=blk),
        grid=(nb, lp // blk),
        in_specs=[pl.BlockSpec((None, blk, HEADS * QM_SLOT), lambda b, i: (b, i, 0)),
                  pl.BlockSpec((None, lp, QM_SLOT), lambda b, i: (b, 0, 0)),
                  _const_spec((HEADS * MLA_KV_RANK, MLA_WIDTH))],
        out_specs=pl.BlockSpec((None, blk, MLA_WIDTH), lambda b, i: (b, i, 0)),
        out_shape=jax.ShapeDtypeStruct((nb, lp, MLA_WIDTH), F32),
        scratch_shapes=[pltpu.VMEM((HEADS, blk, LANES), F32), pltpu.VMEM((HEADS, blk, QM_SLOT), F32)],
        compiler_params=pltpu.CompilerParams(dimension_semantics=("parallel", "parallel"),
                                             vmem_limit_bytes=ATTN_VMEM),
        name="mla_prompt",
    )(qm, kcat, wuv_bd)


def _decode_kernel(pt_ref, qbd_ref, kn_ref, vn_ref, qlat_ref, qrope_ref, cn_ref, rn_ref, wuv_ref,
                   k_hbm, v_hbm, c_hbm, r_hbm, osb_ref, omla_ref,
                   acc_ref, car_ref, m_ref, l_ref, macc_ref, k_buf, v_buf, c_buf, r_buf, sem, *, ppg):
    step = pl.program_id(1)
    nsteps = pl.num_programs(1)
    flat = pl.program_id(0) * nsteps + step
    slot = flat & 1
    caches = ((k_hbm, k_buf), (v_hbm, v_buf), (c_hbm, c_buf), (r_hbm, r_buf))

    def page_copy(f, sl, n, p):
        hbm, buf = caches[n]
        return pltpu.make_async_copy(hbm.at[pt_ref[f * ppg + p]], buf.at[sl, p], sem.at[sl, n])

    def start_fetch(f, sl):
        for n in range(len(caches)):
            for p in range(ppg):
                page_copy(f, sl, n, p).start()

    @pl.when(flat == 0)
    def _():
        start_fetch(flat, slot)

    @pl.when(flat + 1 < pl.num_programs(0) * nsteps)
    def _():
        start_fetch(flat + 1, 1 - slot)

    for n in range(len(caches)):
        for p in range(ppg):
            page_copy(flat, slot, n, p).wait()
    k_pages, v_pages, c_pages, r_pages = ([buf.at[slot, p] for p in range(ppg)] for _, buf in caches)
    rows = SAMPLE_ROWS
    qbd = qbd_ref[...].astype(BF16)
    qlat = qlat_ref[...].astype(BF16)
    qrope = qrope_ref[...].astype(BF16)
    later = _later_key_matrix(PAGE_SIZE)
    later2 = jnp.concatenate([later, later], axis=0)

    def sb_update(z, values, values_transposed, visible):
        n = z.shape[1] // PAGE_SIZE
        page = lambda x, p: x[:, p * PAGE_SIZE:(p + 1) * PAGE_SIZE]
        log_beta, log_keep, hi, lo = _stick_breaking_terms(z, visible)
        stacked = jnp.concatenate(
            [jnp.concatenate([page(hi, p), page(lo, p)], axis=1) for p in range(n)], axis=0)
        local = _dot(stacked, later2)
        carry = car_ref[...]
        after = [None] * n
        for p in reversed(range(n)):
            after[p] = local[p * rows:(p + 1) * rows] + carry
            carry = carry + jnp.sum(page(log_keep, p), axis=-1, keepdims=True)
        w = jnp.exp2(log_beta + jnp.concatenate(after, axis=1))
        if visible is not None:
            w = jnp.where(visible, w, 0.0)
        wb = w.astype(BF16)
        acc_ref[...] += _dot_nt(wb, values) if values_transposed else _dot(wb, values)
        car_ref[...] = carry

    def mla_update(s, latent, visible):
        if visible is not None:
            s = jnp.where(visible, s, NEG_BIG)
        m_old = m_ref[...]
        m_new = jnp.maximum(m_old, jnp.max(s, axis=-1, keepdims=True))
        alpha = jnp.exp2(m_old - m_new)
        p = jnp.exp2(s - m_new)
        l_ref[...] = alpha * l_ref[...] + jnp.sum(p, axis=-1, keepdims=True)
        macc_ref[...] = alpha * macc_ref[...] + _dot(p.astype(BF16), latent)
        m_ref[...] = m_new

    @pl.when(step == 0)
    def _():
        acc_ref[...] = jnp.zeros_like(acc_ref)
        car_ref[...] = jnp.zeros_like(car_ref)
        m_ref[...] = jnp.full_like(m_ref, NEG_BIG)
        l_ref[...] = jnp.zeros_like(l_ref)
        macc_ref[...] = jnp.zeros_like(macc_ref)
        q_idx = lax.broadcasted_iota(jnp.int32, (rows, PAGE_SIZE), 0) >> 3
        col = lax.broadcasted_iota(jnp.int32, (rows, PAGE_SIZE), 1)
        pad = lambda ref: jnp.concatenate(
            [ref[...], jnp.zeros((PAGE_SIZE - ref.shape[0], ref.shape[1]), F32)], axis=0).astype(BF16)
        sb_update(_dot_nt(qbd, pad(kn_ref)), pad(vn_ref), False, col < q_idx)
        latent = pad(cn_ref)
        mla_update(_dot_nt(qlat, latent) + _dot_nt(qrope, pad(rn_ref)), latent, col <= q_idx)

    gather = lambda refs, axis: jnp.concatenate([r[...].astype(BF16) for r in refs], axis=axis)
    sb_update(_dot(qbd, gather(k_pages, 1)), gather(v_pages, 1), True, None)
    latent = gather(c_pages, 0)
    mla_update(_dot_nt(qlat, latent) + _dot(qrope, gather(r_pages, 1)), latent, None)

    @pl.when(step == nsteps - 1)
    def _():
        r = lax.broadcasted_iota(jnp.int32, (rows, SB_WIDTH), 0)
        c = lax.broadcasted_iota(jnp.int32, (rows, SB_WIDTH), 1)
        own = (c >> 6) == (r & 7)
        fold = lambda x: jnp.sum(jnp.where(own, x, 0.0).reshape(rows // HEADS, HEADS, SB_WIDTH), axis=1)
        osb_ref[...] = fold(acc_ref[...])
        o_lat = (macc_ref[...] / l_ref[...]).astype(BF16)
        omla_ref[...] = fold(_dot(o_lat, wuv_ref[...]))


def _decode(page_table, qbd, kn, vn, qlat, qrope, cn, rn, wuv_flat, cache_k, cache_v, cache_c, cache_r):
    nseq, npages = page_table.shape
    ppg = PAGES_PER_STEP
    steps = npages // ppg

    order = page_table.reshape(nseq, steps, ppg)[:, ::-1, :].reshape(-1)
    caches = (cache_k, cache_v, cache_c, cache_r)
    per_seq = lambda r, w: pl.BlockSpec((None, r, w), lambda s, j, pt: (s, 0, 0))
    new_rows = kn.shape[1]
    in_specs = [per_seq(SAMPLE_ROWS, SB_WIDTH), per_seq(new_rows, SB_WIDTH), per_seq(new_rows, SB_WIDTH),
                per_seq(SAMPLE_ROWS, MLA_KV_RANK), per_seq(SAMPLE_ROWS, MLA_ROPE_DIM),
                per_seq(new_rows, MLA_KV_RANK), per_seq(new_rows, MLA_ROPE_DIM),
                pl.BlockSpec((MLA_KV_RANK, MLA_WIDTH), lambda s, j, pt: (0, 0))]
    in_specs += [pl.BlockSpec(memory_space=pl.ANY)] * len(caches)
    page_bufs = [pltpu.VMEM((2, ppg) + c.shape[1:], c.dtype) for c in caches]
    nq = SAMPLE_ROWS // HEADS
    out_spec = pl.BlockSpec((None, nq, SB_WIDTH), lambda s, j, pt: (s, 0, 0))
    return pl.pallas_call(
        functools.partial(_decode_kernel, ppg=ppg),
        grid_spec=pltpu.PrefetchScalarGridSpec(
            num_scalar_prefetch=1,
            grid=(nseq, steps),
            in_specs=in_specs,
            out_specs=[out_spec, out_spec],
            scratch_shapes=[pltpu.VMEM((SAMPLE_ROWS, SB_WIDTH), F32), pltpu.VMEM((SAMPLE_ROWS, 1), F32),
                            pltpu.VMEM((SAMPLE_ROWS, 1), F32), pltpu.VMEM((SAMPLE_ROWS, 1), F32),
                            pltpu.VMEM((SAMPLE_ROWS, MLA_KV_RANK), F32)]
            + page_bufs + [pltpu.SemaphoreType.DMA((2, len(caches)))]),
        out_shape=[jax.ShapeDtypeStruct((nseq, nq, SB_WIDTH), F32), jax.ShapeDtypeStruct((nseq, nq, MLA_WIDTH), F32)],
        compiler_params=pltpu.CompilerParams(dimension_semantics=("arbitrary", "arbitrary"),
                                             vmem_limit_bytes=ATTN_VMEM),
        name="paged_decode",
    )(order, qbd, kn, vn, qlat, qrope, cn, rn, wuv_flat, *caches)


def _rope_tables(pos):
    inv = ROPE_BASE ** (-jnp.arange(0, MLA_ROPE_DIM, 2, dtype=F32) / MLA_ROPE_DIM)
    ang = pos.astype(F32)[:, None] * inv[None, :]
    cos, sin = jnp.cos(ang), jnp.sin(ang)
    c32 = jnp.concatenate([cos, cos], axis=1)
    s32 = jnp.concatenate([-sin, sin], axis=1)
    pad = jnp.zeros((pos.shape[0], LANES - MLA_ROPE_DIM), F32)
    return (jnp.tile(c32, (1, HEADS)), jnp.tile(s32, (1, HEADS)),
            jnp.concatenate([c32, pad], axis=1), jnp.concatenate([s32, pad], axis=1))


def kernel(x_prompt, x_sample, cache_sb_k, cache_sb_v, cache_mla_ckv, cache_mla_krope, page_table, meta_tokens,
           ffn1_w_gu, ffn1_w_down, ln1_g, ln1_b, w_in, q_norm_g, w_uq, kv_norm_g, w_uk, w_uv, sb_out_g, mla_out_g,
           w_o, ln2_g, ln2_b, ffn2_w_gu, ffn2_w_down, ln3_g, ln3_b):
    assert w_in.shape[0] == DEPTH
    nb, seq, _ = x_prompt.shape
    nseq, dec_seq, _ = x_sample.shape
    ctx = seq + N_META
    lp = -(-ctx // ATT_BLOCK) * ATT_BLOCK
    past = page_table.shape[1] * PAGE_SIZE
    eye = jnp.eye(HEADS, dtype=F32)

    def ffn_weights(w_gu, w_down):
        return w_gu[0, :, :D_FF].astype(BF16), w_gu[0, :, D_FF:].astype(BF16), w_down[0].astype(BF16)

    ffn1 = ffn_weights(ffn1_w_gu, ffn1_w_down)
    ffn2 = ffn_weights(ffn2_w_gu, ffn2_w_down)
    win = jnp.pad(w_in[0], ((0, 0), (0, IN_PAD - IN_WIDTH))).astype(BF16)
    qk = MLA_NOPE_DIM + MLA_ROPE_DIM
    head_base = jnp.arange(HEADS)[:, None] * qk
    perm = jnp.concatenate([(head_base + jnp.arange(MLA_NOPE_DIM)[None, :]).reshape(-1),
                            (head_base + MLA_NOPE_DIM + jnp.arange(MLA_ROPE_DIM)[None, :]).reshape(-1)])
    wuq = w_uq[0][:, perm].astype(BF16)
    wuk_t = jnp.transpose(w_uk[0], (1, 2, 0))
    wuk_bd = (wuk_t[:, :, None, :] * eye[:, None, :, None]).reshape(SB_WIDTH, HEADS * MLA_KV_RANK).astype(BF16)
    wuv_t = jnp.transpose(w_uv[0], (1, 0, 2))
    wuv_bd = (wuv_t[:, :, None, :] * eye[:, None, :, None]).reshape(HEADS * MLA_KV_RANK, MLA_WIDTH).astype(BF16)
    wuv_flat = w_uv[0].reshape(MLA_KV_RANK, MLA_WIDTH).astype(BF16)
    sel_src = jnp.arange(HEADS * MLA_ROPE_DIM)
    sel_dst = (sel_src // MLA_ROPE_DIM) * LANES + sel_src % MLA_ROPE_DIM
    psel = jnp.zeros((HEADS * MLA_ROPE_DIM, HEADS * LANES), F32).at[sel_src, sel_dst].set(1.0).astype(BF16)
    wo_sb, wo_mla = w_o[0, :SB_WIDTH].astype(BF16), w_o[0, SB_WIDTH:].astype(BF16)

    meta = jnp.broadcast_to(meta_tokens[None].astype(x_prompt.dtype), (nb, N_META, D_MODEL))
    xp = jnp.concatenate([meta, x_prompt, jnp.zeros((nb, lp - ctx, D_MODEL), x_prompt.dtype)], axis=1)
    xp = xp.reshape(nb * lp, D_MODEL)
    xs = x_sample.reshape(nseq * dec_seq, D_MODEL)
    tab_p = _rope_tables(jnp.arange(lp))
    tab_s = _rope_tables(jnp.tile(past + jnp.arange(dec_seq), nseq))

    def front(x, batches, kv_positions, tabs):
        h1 = _ffn_ln(x, *ffn1, ln1_g, ln1_b)
        return h1, _proj(h1, batches, kv_positions, win, q_norm_g, wuq, kv_norm_g, wuk_bd, psel, *tabs)

    def back(h1, osb, omla):
        return _mix_ffn(h1, osb, omla, sb_out_g, mla_out_g, wo_sb, wo_mla, ln2_g, ln2_b, *ffn2, ln3_g, ln3_b)

    h1p, (kp, vp, cp, rp, qsb_p, ksb_p, vsb_p, qm_p, kcat_p) = front(xp, nb, ctx, tab_p)
    b3 = lambda a: a.reshape(nb, lp, a.shape[-1])
    osb_p = _sb_prompt(b3(qsb_p), b3(ksb_p), b3(vsb_p))
    omla_p = _mla_prompt(b3(qm_p), b3(kcat_p), wuv_bd)
    yp = back(h1p, osb_p.reshape(nb * lp, SB_WIDTH), omla_p.reshape(nb * lp, MLA_WIDTH))

    h1s, (ks, vs, cs, rs, qsb_s, _, _, qm_s, _) = front(xs, 1, None, tab_s)
    head_of_lane = jnp.arange(SB_WIDTH) // SB_HEAD_DIM
    head_mask = (head_of_lane[None, :] == jnp.arange(HEADS)[:, None]).astype(F32)
    qbd = (qsb_s.astype(F32).reshape(nseq, dec_seq, 1, SB_WIDTH) * head_mask[None, None]).reshape(
        nseq, dec_seq * HEADS, SB_WIDTH)
    qm4 = qm_s.astype(F32).reshape(nseq, dec_seq * HEADS, QM_SLOT)
    qlat_s = qm4[:, :, :MLA_KV_RANK]
    qrope_s = qm4[:, :, MLA_KV_RANK:MLA_KV_RANK + MLA_ROPE_DIM]
    new8 = lambda a: jnp.pad(a.reshape(nseq, dec_seq, a.shape[-1]), ((0, 0), (0, 8 - dec_seq), (0, 0)))
    feature_major = lambda c: jnp.moveaxis(c[0], 1, -1).reshape(c.shape[1], -1, PAGE_SIZE)
    osb_s, omla_s = _decode(
        page_table, qbd, new8(ks), new8(vs), qlat_s, qrope_s, new8(cs), new8(rs), wuv_flat,
        feature_major(cache_sb_k), feature_major(cache_sb_v), cache_mla_ckv[0], feature_major(cache_mla_krope))
    ys = back(h1s, osb_s.reshape(nseq * dec_seq, SB_WIDTH), omla_s.reshape(nseq * dec_seq, MLA_WIDTH))

    prompt = lambda a, shape: a.reshape((nb, lp) + shape)[:, :ctx][None]
    sample = lambda a, shape: a.reshape((nseq, dec_seq) + shape)[None]
    hd = (HEADS, SB_HEAD_DIM)
    prompt_kv = lambda a: jnp.moveaxis(a.reshape((nb,) + hd + (ctx,)), -1, 1)[None]
    return (yp.reshape(nb, lp, D_MODEL)[:, N_META:ctx], ys.reshape(nseq, dec_seq, D_MODEL),
            prompt_kv(kp), prompt_kv(vp), prompt(cp, (MLA_KV_RANK,)), prompt(rp, (MLA_ROPE_DIM,)),
            sample(ks, hd), sample(vs, hd), sample(cs, (MLA_KV_RANK,)), sample(rs, (MLA_ROPE_DIM,)))
```

```python
import functools

import jax
import jax.numpy as jnp
from jax import lax
from jax.experimental import pallas as pl
from jax.experimental.pallas import tpu as pltpu

F32 = jnp.float32
BF16 = jnp.bfloat16

D_MODEL = 1024
N_META = 16
HEADS = 8
SB_HEAD_DIM = 64
SB_WIDTH = HEADS * SB_HEAD_DIM
MLA_NOPE_DIM = 64
MLA_ROPE_DIM = 32
MLA_V_DIM = 64
MLA_WIDTH = HEADS * MLA_V_DIM
MLA_Q_RANK = 256
MLA_KV_RANK = 128
D_FF = 2816
PAGE_SIZE = 128
ROPE_BASE = 10000.0
LN_EPS = 1e-5
RMS_EPS = 1e-6
DEPTH = 1
DEEPNORM_ALPHA = (2 * DEPTH) ** 0.25
LOG2E = 1.4426950408889634
SB_QSCALE = SB_HEAD_DIM ** -0.5 * LOG2E
MLA_QSCALE = (MLA_NOPE_DIM + MLA_ROPE_DIM) ** -0.5 * LOG2E
IN_WIDTH = 3 * SB_WIDTH + MLA_Q_RANK + MLA_KV_RANK + MLA_ROPE_DIM

LANES = 128
MXU_DIM = 256
IN_PAD = 2048
QM_SLOT = 256
ONES_LANE = MLA_KV_RANK + MLA_ROPE_DIM
NEG_BIG = -1e30

ROW_TILE = 512
FF_CHUNK = 256
ATT_BLOCK = 256
SB_CHAIN_ROWS = 2048
SB_DEAD_LOG2 = -160.0
PAGES_PER_STEP = 16
PAGE_RING = 3
SAMPLE_ROWS = HEADS * 4
ROWWISE_VMEM = 56 * 1024 * 1024
ATTN_VMEM = 48 * 1024 * 1024


def _const_spec(shape):
    zeros = (0,) * len(shape)
    return pl.BlockSpec(shape, lambda *_: zeros, pipeline_mode=pl.Buffered(1))


def _layer_norm(y, g, b):
    mu = jnp.mean(y, axis=-1, keepdims=True)
    d = y - mu
    var = jnp.mean(d * d, axis=-1, keepdims=True)
    return d * lax.rsqrt(var + LN_EPS) * g + b


def _rms_norm(y, g):
    return y * lax.rsqrt(jnp.mean(y * y, axis=-1, keepdims=True) + RMS_EPS) * g


def _dot(a, b):
    return jnp.dot(a, b, preferred_element_type=F32)


def _dot_nt(a, b):
    return lax.dot_general(a, b, (((1,), (1,)), ((), ())), preferred_element_type=F32)


def _swiglu(xb, wg_ref, wu_ref, wd_ref, h_ref):
    for c in range(0, D_FF, FF_CHUNK):
        g = _dot(xb, wg_ref[:, c:c + FF_CHUNK])
        u = _dot(xb, wu_ref[:, c:c + FF_CHUNK])
        h_ref[:, c:c + FF_CHUNK] = (g * jax.nn.sigmoid(g) * u).astype(BF16)
    return _dot(h_ref[...], wd_ref[...])


def _rope_lanes(x, c, s):
    lane = lax.broadcasted_iota(jnp.int32, x.shape, 1)
    swapped = jnp.where((lane & 31) < 16, pltpu.roll(x, LANES - 16, 1), pltpu.roll(x, 16, 1))
    return x * c + swapped * s


def _ffn_ln_kernel(x_ref, wg_ref, wu_ref, wd_ref, g_ref, b_ref, o_ref, h_ref):
    x = x_ref[...]
    f = _swiglu(x.astype(BF16), wg_ref, wu_ref, wd_ref, h_ref)
    o_ref[...] = _layer_norm(DEEPNORM_ALPHA * x + 0.5 * f, g_ref[...], b_ref[...])


def _ffn_ln(x, wg, wu, wd, g, b):
    rows = x.shape[0]
    tm = min(ROW_TILE, rows)
    return pl.pallas_call(
        _ffn_ln_kernel,
        grid=(rows // tm,),
        in_specs=[pl.BlockSpec((tm, D_MODEL), lambda i: (i, 0)),
                  _const_spec((D_MODEL, D_FF)), _const_spec((D_MODEL, D_FF)), _const_spec((D_FF, D_MODEL)),
                  _const_spec((1, D_MODEL)), _const_spec((1, D_MODEL))],
        out_specs=pl.BlockSpec((tm, D_MODEL), lambda i: (i, 0)),
        out_shape=jax.ShapeDtypeStruct((rows, D_MODEL), F32),
        scratch_shapes=[pltpu.VMEM((tm, D_FF), BF16)],
        compiler_params=pltpu.CompilerParams(dimension_semantics=("parallel",), vmem_limit_bytes=ROWWISE_VMEM),
        name="ffn_ln",
    )(x, wg, wu, wd, g, b)


def _proj_kernel(h_ref, win_ref, qg_ref, wuq_ref, kvg_ref, wuk_ref, psel_ref, cq_ref, sq_ref, ck_ref, sk_ref,
                 k_out, v_out, ckv_out, kr_out, qsb_out, ksb_out, vsb_out, qm_out, kcat_out, *, kv_position_minor):
    p = _dot(h_ref[...].astype(BF16), win_ref[...])
    sq = p[:, 0:SB_WIDTH]
    sk = p[:, SB_WIDTH:2 * SB_WIDTH]
    sv = p[:, 2 * SB_WIDTH:3 * SB_WIDTH]
    cq = p[:, 3 * SB_WIDTH:3 * SB_WIDTH + MLA_Q_RANK]
    ckv = p[:, 3 * SB_WIDTH + MLA_Q_RANK:IN_PAD - LANES]
    krp = p[:, IN_PAD - LANES:]
    k_out[...] = sk.T if kv_position_minor else sk
    v_out[...] = sv.T if kv_position_minor else sv
    qsb_out[...] = (sq * SB_QSCALE).astype(BF16)
    ksb_out[...] = sk.astype(BF16)
    vsb_out[...] = sv.astype(BF16)

    q = _dot(_rms_norm(cq, qg_ref[...]).astype(BF16), wuq_ref[...])
    qlat = _dot(q[:, :SB_WIDTH].astype(BF16), wuk_ref[...]) * MLA_QSCALE
    qr = q[:, SB_WIDTH:] * MLA_QSCALE
    qro = jnp.concatenate(
        [_rope_lanes(qr[:, :LANES], cq_ref[:, :LANES], sq_ref[:, :LANES]),
         _rope_lanes(qr[:, LANES:], cq_ref[:, LANES:], sq_ref[:, LANES:])], axis=1)
    qrs = _dot(qro.astype(BF16), psel_ref[...])
    for h in range(HEADS):
        qm_out[:, h * QM_SLOT:h * QM_SLOT + LANES] = qlat[:, h * LANES:(h + 1) * LANES].astype(BF16)
        qm_out[:, h * QM_SLOT + LANES:(h + 1) * QM_SLOT] = qrs[:, h * LANES:(h + 1) * LANES].astype(BF16)

    ckvn = _rms_norm(ckv, kvg_ref[...])
    ckv_out[...] = ckvn
    kro = _rope_lanes(krp, ck_ref[...], sk_ref[...])
    kr_out[...] = kro[:, :MLA_ROPE_DIM]
    kcat_out[:, :LANES] = ckvn.astype(BF16)
    lane = lax.broadcasted_iota(jnp.int32, kro.shape, 1)
    kcat_out[:, LANES:] = jnp.where(lane == ONES_LANE - LANES, 1.0, kro).astype(BF16)


def _proj(h, nb, kv_positions, win, qg, wuq, kvg, wuk_bd, psel, cq, sq, ck, sk):
    rows = h.shape[0]
    tm = min(ATT_BLOCK, rows)
    nblk = rows // nb // tm
    row = lambda w: pl.BlockSpec((tm, w), lambda b, i: (b * nblk + i, 0))
    tab = lambda w: pl.BlockSpec((tm, w), lambda b, i: (i, 0))
    widths_dtypes = [(MLA_KV_RANK, F32), (MLA_ROPE_DIM, F32), (SB_WIDTH, BF16), (SB_WIDTH, BF16), (SB_WIDTH, BF16),
                     (HEADS * QM_SLOT, BF16), (2 * LANES, BF16)]
    if kv_positions is None:
        kv_spec, kv_shape = row(SB_WIDTH), jax.ShapeDtypeStruct((rows, SB_WIDTH), F32)
    else:
        kv_spec = pl.BlockSpec((None, SB_WIDTH, tm), lambda b, i: (b, 0, i))
        kv_shape = jax.ShapeDtypeStruct((nb, SB_WIDTH, kv_positions), F32)
    return pl.pallas_call(
        functools.partial(_proj_kernel, kv_position_minor=kv_positions is not None),
        grid=(nb, nblk),
        in_specs=[row(D_MODEL), _const_spec((D_MODEL, IN_PAD)), _const_spec((1, MLA_Q_RANK)),
                  _const_spec((MLA_Q_RANK, HEADS * (MLA_NOPE_DIM + MLA_ROPE_DIM))), _const_spec((1, MLA_KV_RANK)),
                  _const_spec((SB_WIDTH, HEADS * LANES)), _const_spec((2 * LANES, HEADS * LANES)),
                  tab(2 * LANES), tab(2 * LANES), tab(LANES), tab(LANES)],
        out_specs=[kv_spec, kv_spec] + [row(w) for w, _ in widths_dtypes],
        out_shape=[kv_shape, kv_shape] + [jax.ShapeDtypeStruct((rows, w), d) for w, d in widths_dtypes],
        compiler_params=pltpu.CompilerParams(dimension_semantics=("parallel", "parallel"),
                                             vmem_limit_bytes=ROWWISE_VMEM),
        name="mixer_proj",
    )(h, win, qg, wuq, kvg, wuk_bd, psel, cq, sq, ck, sk)


def _mix_ffn_kernel(h_ref, osb_ref, omla_ref, sbg_ref, mlag_ref, wo_sb_ref, wo_mla_ref, g2_ref, b2_ref,
                    wg_ref, wu_ref, wd_ref, g3_ref, b3_ref, o_ref, hs_ref):
    a = _rms_norm(osb_ref[...], sbg_ref[...]).astype(BF16)
    b = _rms_norm(omla_ref[...], mlag_ref[...]).astype(BF16)
    mixed = _dot(a, wo_sb_ref[...]) + _dot(b, wo_mla_ref[...])
    h2 = _layer_norm(DEEPNORM_ALPHA * h_ref[...] + mixed, g2_ref[...], b2_ref[...])
    f = _swiglu(h2.astype(BF16), wg_ref, wu_ref, wd_ref, hs_ref)
    o_ref[...] = _layer_norm(DEEPNORM_ALPHA * h2 + 0.5 * f, g3_ref[...], b3_ref[...])


def _mix_ffn(h, osb, omla, sbg, mlag, wo_sb, wo_mla, g2, b2, wg, wu, wd, g3, b3):
    rows = h.shape[0]
    tm = min(ROW_TILE, rows)
    row = lambda w: pl.BlockSpec((tm, w), lambda i: (i, 0))
    vec = lambda w: _const_spec((1, w))
    return pl.pallas_call(
        _mix_ffn_kernel,
        grid=(rows // tm,),
        in_specs=[row(D_MODEL), row(SB_WIDTH), row(MLA_WIDTH), vec(SB_WIDTH), vec(MLA_WIDTH),
                  _const_spec((SB_WIDTH, D_MODEL)), _const_spec((MLA_WIDTH, D_MODEL)), vec(D_MODEL), vec(D_MODEL),
                  _const_spec((D_MODEL, D_FF)), _const_spec((D_MODEL, D_FF)), _const_spec((D_FF, D_MODEL)),
                  vec(D_MODEL), vec(D_MODEL)],
        out_specs=row(D_MODEL),
        out_shape=jax.ShapeDtypeStruct((rows, D_MODEL), F32),
        scratch_shapes=[pltpu.VMEM((tm, D_FF), BF16)],
        compiler_params=pltpu.CompilerParams(dimension_semantics=("parallel",), vmem_limit_bytes=ROWWISE_VMEM),
        name="mix_ffn",
    )(h, osb, omla, sbg, mlag, wo_sb, wo_mla, g2, b2, wg, wu, wd, g3, b3)


def _later_key_matrix(n):
    r = lax.broadcasted_iota(jnp.int32, (n, n), 0)
    c = lax.broadcasted_iota(jnp.int32, (n, n), 1)
    return jnp.where(r > c, 1.0, 0.0).astype(BF16)


def _stick_breaking_terms(z, visible):
    t = jnp.log(1.0 + jnp.exp2(-jnp.abs(z))) * LOG2E
    log_beta = jnp.minimum(z, 0.0) - t
    log_keep = log_beta - z
    if visible is not None:
        log_keep = jnp.where(visible, log_keep, 0.0)
    hi = log_keep.astype(BF16)
    lo = (log_keep - hi.astype(F32)).astype(BF16)
    return log_beta, log_keep, hi, lo


def _lane_tiled(x, width):
    return jnp.concatenate([x] * (width // LANES), axis=1) if width > LANES else x


def _sb_prompt_kernel(q_ref, k_ref, v_ref, o_ref, qh_ref, acc_ref, car_ref, *, blk):
    qi = pl.program_id(1)
    pairs = SB_WIDTH // LANES
    pair_rows = 2 * blk
    lane = lax.broadcasted_iota(jnp.int32, (blk, LANES), 1)
    for hp in range(pairs):
        q2 = q_ref[:, hp * LANES:(hp + 1) * LANES]
        qh_ref[pl.ds(hp * pair_rows, blk)] = jnp.where(lane < SB_HEAD_DIM, q2, jnp.zeros_like(q2))
        qh_ref[pl.ds(hp * pair_rows + blk, blk)] = jnp.where(lane >= SB_HEAD_DIM, q2, jnp.zeros_like(q2))
    later = _later_key_matrix(blk)
    later2 = jnp.concatenate([later, later], axis=0)
    rowq = lax.broadcasted_iota(jnp.int32, (blk, blk), 0)
    colk = lax.broadcasted_iota(jnp.int32, (blk, blk), 1)
    acc_ref[...] = jnp.zeros_like(acc_ref)
    car_ref[...] = jnp.zeros_like(car_ref)
    chain_pairs = SB_CHAIN_ROWS // pair_rows

    def block(kj, visible):
        off = pl.multiple_of(kj * blk, blk)
        if visible is not None:
            visible = jnp.concatenate([visible] * (SB_CHAIN_ROWS // blk), axis=0)
        for c in range(pairs // chain_pairs):
            chain = range(c * chain_pairs, (c + 1) * chain_pairs)
            rows = pl.ds(c * SB_CHAIN_ROWS, SB_CHAIN_ROWS)
            z = jnp.concatenate(
                [_dot_nt(qh_ref[pl.ds(hp * pair_rows, pair_rows)], k_ref[pl.ds(off, blk), hp * LANES:(hp + 1) * LANES])
                 for hp in chain], axis=0)
            log_beta, log_keep, hi, lo = _stick_breaking_terms(z, visible)
            carry = car_ref[rows]
            after = _dot(jnp.concatenate([hi, lo], axis=1), later2) + _lane_tiled(carry, blk)
            w = jnp.exp2(log_beta + after)
            if visible is not None:
                w = jnp.where(visible, w, 0.0)
            wb = w.astype(BF16)
            for n, hp in enumerate(chain):
                acc_ref[pl.ds(hp * pair_rows, pair_rows)] += _dot(
                    wb[n * pair_rows:(n + 1) * pair_rows], v_ref[pl.ds(off, blk), hp * LANES:(hp + 1) * LANES])
            car_ref[rows] = carry + jnp.sum(log_keep, axis=-1, keepdims=True)

    block(qi, colk < rowq)

    def body(state):
        i, _ = state
        block(qi - 1 - i, None)
        return i + 1, (jnp.max(car_ref[...]) > SB_DEAD_LOG2).astype(jnp.int32)

    lax.while_loop(lambda state: (state[0] < qi) & (state[1] > 0), body, (jnp.int32(0), jnp.int32(1)))
    for hp in range(pairs):
        o_ref[:, hp * LANES:(hp + 1) * LANES] = jnp.where(
            lane < SB_HEAD_DIM, acc_ref[pl.ds(hp * pair_rows, blk)], acc_ref[pl.ds(hp * pair_rows + blk, blk)])


def _sb_prompt(q, k, v):
    nb, lp, _ = q.shape
    blk = ATT_BLOCK
    resident = pl.BlockSpec((None, lp, SB_WIDTH), lambda b, i: (b, 0, 0), pipeline_mode=pl.Buffered(1))
    return pl.pallas_call(
        functools.partial(_sb_prompt_kernel, blk=blk),
        grid=(nb, lp // blk),
        in_specs=[pl.BlockSpec((None, blk, SB_WIDTH), lambda b, i: (b, i, 0)), resident, resident],
        out_specs=pl.BlockSpec((None, blk, SB_WIDTH), lambda b, i: (b, i, 0)),
        out_shape=jax.ShapeDtypeStruct((nb, lp, SB_WIDTH), F32),
        scratch_shapes=[pltpu.VMEM((HEADS * blk, LANES), BF16), pltpu.VMEM((HEADS * blk, LANES), F32),
                        pltpu.VMEM((HEADS * blk, LANES), F32)],
        compiler_params=pltpu.CompilerParams(dimension_semantics=("parallel", "parallel"),
                                             vmem_limit_bytes=ATTN_VMEM),
        name="sb_prompt",
    )(q, k, v)


def _mla_prompt_kernel(q_ref, kc_ref, wuv_ref, o_ref, m_ref, acc_ref, *, blk):
    qi = pl.program_id(1)
    rowq = lax.broadcasted_iota(jnp.int32, (blk, blk), 0)
    colk = lax.broadcasted_iota(jnp.int32, (blk, blk), 1)
    m_ref[...] = jnp.full_like(m_ref, NEG_BIG)
    acc_ref[...] = jnp.zeros_like(acc_ref)

    def block(off, nkeys, visible):
        kc = kc_ref[pl.ds(pl.multiple_of(off, blk), nkeys), :]
        for h in range(HEADS):
            s = _dot_nt(q_ref[:, h * QM_SLOT:(h + 1) * QM_SLOT], kc)
            if visible is not None:
                s = jnp.where(visible, s, NEG_BIG)
            m_old = m_ref[h]
            m_new = jnp.maximum(m_old, jnp.max(s, axis=-1, keepdims=True))
            p = jnp.exp2(s - _lane_tiled(m_new, nkeys))
            acc_ref[h] = _lane_tiled(jnp.exp2(m_old - m_new), QM_SLOT) * acc_ref[h] + _dot(p.astype(BF16), kc)
            m_ref[h] = m_new

    block(qi * blk, blk, colk <= rowq)

    def body(i, c):
        block(i * (2 * blk), 2 * blk, None)
        return c

    lax.fori_loop(0, lax.shift_right_logical(qi, 1), body, 0)

    @pl.when((qi & 1) == 1)
    def _():
        block((qi - 1) * blk, blk, None)

    out = jnp.zeros((blk, MLA_WIDTH), F32)
    for h in range(HEADS):
        a = acc_ref[h]
        o_lat = (a[:, :MLA_KV_RANK] / a[:, ONES_LANE:ONES_LANE + 1]).astype(BF16)
        out = out + _dot(o_lat, wuv_ref[h * MLA_KV_RANK:(h + 1) * MLA_KV_RANK, :])
    o_ref[...] = out


def _mla_prompt(qm, kcat, wuv_bd):
    nb, lp, _ = qm.shape
    blk = ATT_BLOCK
    return pl.pallas_call(
        functools.partial(_mla_prompt_kernel, blk=blk),
        grid=(nb, lp // blk),
        in_specs=[pl.BlockSpec((None, blk, HEADS * QM_SLOT), lambda b, i: (b, i, 0)),
                  pl.BlockSpec((None, lp, QM_SLOT), lambda b, i: (b, 0, 0)),
                  _const_spec((HEADS * MLA_KV_RANK, MLA_WIDTH))],
        out_specs=pl.BlockSpec((None, blk, MLA_WIDTH), lambda b, i: (b, i, 0)),
        out_shape=jax.ShapeDtypeStruct((nb, lp, MLA_WIDTH), F32),
        scratch_shapes=[pltpu.VMEM((HEADS, blk, LANES), F32), pltpu.VMEM((HEADS, blk, QM_SLOT), F32)],
        compiler_params=pltpu.CompilerParams(dimension_semantics=("parallel", "parallel"),
                                             vmem_limit_bytes=ATTN_VMEM),
        name="mla_prompt",
    )(qm, kcat, wuv_bd)


def _decode_kernel(pt_ref, qbd_ref, kn_ref, vn_ref, qlat_ref, qrope_ref, cn_ref, rn_ref, wuv_ref,
                   k_hbm, v_hbm, c_hbm, r_hbm, osb_ref, omla_ref,
                   acc_ref, car_ref, m_ref, l_ref, macc_ref, k_buf, v_buf, c_buf, r_buf, sem,
                   alive_ref, kv_fetched_ref, *, ppg):
    step = pl.program_id(1)
    nsteps = pl.num_programs(1)
    flat = pl.program_id(0) * nsteps + step
    total = pl.num_programs(0) * nsteps
    slot = lax.rem(flat, PAGE_RING)
    caches = ((k_hbm, k_buf), (v_hbm, v_buf), (c_hbm, c_buf), (r_hbm, r_buf))
    kv, latent_rope = (0, 1), (2, 3)

    @pl.when(step == 0)
    def _():
        alive_ref[0] = 1

    def page_copy(f, sl, n, p):
        hbm, buf = caches[n]
        return pltpu.make_async_copy(hbm.at[pt_ref[f * ppg + p]], buf.at[sl, p], sem.at[sl, n])

    def start_fetch(f, sl, which):
        for n in which:
            for p in range(ppg):
                page_copy(f, sl, n, p).start()

    def wait_fetch(which):
        for n in which:
            for p in range(ppg):
                page_copy(flat, slot, n, p).wait()

    @pl.when(flat == 0)
    def _():
        for ahead in range(PAGE_RING - 1):
            start_fetch(ahead, ahead, kv + latent_rope)
            kv_fetched_ref[ahead] = 1

    ahead = flat + (PAGE_RING - 1)
    ahead_slot = lax.rem(ahead, PAGE_RING)

    @pl.when(ahead < total)
    def _():
        start_fetch(ahead, ahead_slot, latent_rope)
        dead = (step + (PAGE_RING - 1) < nsteps) & (step >= 1) & (alive_ref[0] == 0)
        kv_fetched_ref[ahead_slot] = jnp.where(dead, 0, 1)

        @pl.when(jnp.logical_not(dead))
        def _():
            start_fetch(ahead, ahead_slot, kv)

    wait_fetch(latent_rope)

    @pl.when(kv_fetched_ref[slot] == 1)
    def _():
        wait_fetch(kv)

    k_pages, v_pages, c_pages, r_pages = ([buf.at[slot, p] for p in range(ppg)] for _, buf in caches)
    rows = SAMPLE_ROWS
    qbd = qbd_ref[...].astype(BF16)
    qlat = qlat_ref[...].astype(BF16)
    qrope = qrope_ref[...].astype(BF16)
    later = _later_key_matrix(PAGE_SIZE)
    later2 = jnp.concatenate([later, later], axis=0)

    def sb_update(z, values, values_transposed, visible):
        n = z.shape[1] // PAGE_SIZE
        page = lambda x, p: x[:, p * PAGE_SIZE:(p + 1) * PAGE_SIZE]
        log_beta, log_keep, hi, lo = _stick_breaking_terms(z, visible)
        stacked = jnp.concatenate(
            [jnp.concatenate([page(hi, p), page(lo, p)], axis=1) for p in range(n)], axis=0)
        local = _dot(stacked, later2)
        carry = car_ref[...]
        after = [None] * n
        for p in reversed(range(n)):
            after[p] = local[p * rows:(p + 1) * rows] + carry
            carry = carry + jnp.sum(page(log_keep, p), axis=-1, keepdims=True)
        w = jnp.exp2(log_beta + jnp.concatenate(after, axis=1))
        if visible is not None:
            w = jnp.where(visible, w, 0.0)
        wb = w.astype(BF16)
        acc_ref[...] += _dot_nt(wb, values) if values_transposed else _dot(wb, values)
        car_ref[...] = carry

    def mla_update(s, latent, visible):
        if visible is not None:
            s = jnp.where(visible, s, NEG_BIG)
        m_old = m_ref[...]
        m_new = jnp.maximum(m_old, jnp.max(s, axis=-1, keepdims=True))
        alpha = jnp.exp2(m_old - m_new)
        p = jnp.exp2(s - m_new)
        l_ref[...] = alpha * l_ref[...] + jnp.sum(p, axis=-1, keepdims=True)
        macc_ref[...] = alpha * macc_ref[...] + _dot(p.astype(BF16), latent)
        m_ref[...] = m_new

    @pl.when(step == 0)
    def _():
        acc_ref[...] = jnp.zeros_like(acc_ref)
        car_ref[...] = jnp.zeros_like(car_ref)
        m_ref[...] = jnp.full_like(m_ref, NEG_BIG)
        l_ref[...] = jnp.zeros_like(l_ref)
        macc_ref[...] = jnp.zeros_like(macc_ref)
        q_idx = lax.broadcasted_iota(jnp.int32, (rows, PAGE_SIZE), 0) >> 3
        col = lax.broadcasted_iota(jnp.int32, (rows, PAGE_SIZE), 1)
        pad = lambda ref: jnp.concatenate(
            [ref[...], jnp.zeros((PAGE_SIZE - ref.shape[0], ref.shape[1]), F32)], axis=0).astype(BF16)
        sb_update(_dot_nt(qbd, pad(kn_ref)), pad(vn_ref), False, col < q_idx)
        latent = pad(cn_ref)
        mla_update(_dot_nt(qlat, latent) + _dot_nt(qrope, pad(rn_ref)), latent, col <= q_idx)

    gather = lambda refs, axis: jnp.concatenate([r[...].astype(BF16) for r in refs], axis=axis)

    @pl.when(alive_ref[0] == 1)
    def _():
        sb_update(_dot(qbd, gather(k_pages, 1)), gather(v_pages, 1), True, None)
        alive_ref[0] = (jnp.max(car_ref[...]) > SB_DEAD_LOG2).astype(jnp.int32)

    latent = gather(c_pages, 0)
    mla_update(_dot_nt(qlat, latent) + _dot(qrope, gather(r_pages, 1)), latent, None)

    @pl.when(step == nsteps - 1)
    def _():
        r = lax.broadcasted_iota(jnp.int32, (rows, SB_WIDTH), 0)
        c = lax.broadcasted_iota(jnp.int32, (rows, SB_WIDTH), 1)
        own = (c >> 6) == (r & 7)
        fold = lambda x: jnp.sum(jnp.where(own, x, 0.0).reshape(rows // HEADS, HEADS, SB_WIDTH), axis=1)
        osb_ref[...] = fold(acc_ref[...])
        o_lat = (macc_ref[...] / l_ref[...]).astype(BF16)
        omla_ref[...] = fold(_dot(o_lat, wuv_ref[...]))


def _decode(page_table, qbd, kn, vn, qlat, qrope, cn, rn, wuv_flat, cache_k, cache_v, cache_c, cache_r):
    nseq, npages = page_table.shape
    ppg = PAGES_PER_STEP
    steps = npages // ppg

    order = page_table.reshape(nseq, steps, ppg)[:, ::-1, :].reshape(-1)
    caches = (cache_k, cache_v, cache_c, cache_r)
    per_seq = lambda r, w: pl.BlockSpec((None, r, w), lambda s, j, pt: (s, 0, 0))
    new_rows = kn.shape[1]
    in_specs = [per_seq(SAMPLE_ROWS, SB_WIDTH), per_seq(new_rows, SB_WIDTH), per_seq(new_rows, SB_WIDTH),
                per_seq(SAMPLE_ROWS, MLA_KV_RANK), per_seq(SAMPLE_ROWS, MLA_ROPE_DIM),
                per_seq(new_rows, MLA_KV_RANK), per_seq(new_rows, MLA_ROPE_DIM),
                pl.BlockSpec((MLA_KV_RANK, MLA_WIDTH), lambda s, j, pt: (0, 0))]
    in_specs += [pl.BlockSpec(memory_space=pl.ANY)] * len(caches)
    assert nseq * steps >= PAGE_RING - 1
    page_bufs = [pltpu.VMEM((PAGE_RING, ppg) + c.shape[1:], c.dtype) for c in caches]
    nq = SAMPLE_ROWS // HEADS
    out_spec = pl.BlockSpec((None, nq, SB_WIDTH), lambda s, j, pt: (s, 0, 0))
    return pl.pallas_call(
        functools.partial(_decode_kernel, ppg=ppg),
        grid_spec=pltpu.PrefetchScalarGridSpec(
            num_scalar_prefetch=1,
            grid=(nseq, steps),
            in_specs=in_specs,
            out_specs=[out_spec, out_spec],
            scratch_shapes=[pltpu.VMEM((SAMPLE_ROWS, SB_WIDTH), F32), pltpu.VMEM((SAMPLE_ROWS, 1), F32),
                            pltpu.VMEM((SAMPLE_ROWS, 1), F32), pltpu.VMEM((SAMPLE_ROWS, 1), F32),
                            pltpu.VMEM((SAMPLE_ROWS, MLA_KV_RANK), F32)]
            + page_bufs + [pltpu.SemaphoreType.DMA((PAGE_RING, len(caches))),
                           pltpu.SMEM((1,), jnp.int32), pltpu.SMEM((PAGE_RING,), jnp.int32)]),
        out_shape=[jax.ShapeDtypeStruct((nseq, nq, SB_WIDTH), F32), jax.ShapeDtypeStruct((nseq, nq, MLA_WIDTH), F32)],
        compiler_params=pltpu.CompilerParams(dimension_semantics=("arbitrary", "arbitrary"),
                                             vmem_limit_bytes=ATTN_VMEM),
        name="paged_decode",
    )(order, qbd, kn, vn, qlat, qrope, cn, rn, wuv_flat, *caches)


def _rope_tables(pos):
    inv = ROPE_BASE ** (-jnp.arange(0, MLA_ROPE_DIM, 2, dtype=F32) / MLA_ROPE_DIM)
    ang = pos.astype(F32)[:, None] * inv[None, :]
    cos, sin = jnp.cos(ang), jnp.sin(ang)
    c32 = jnp.concatenate([cos, cos], axis=1)
    s32 = jnp.concatenate([-sin, sin], axis=1)
    pad = jnp.zeros((pos.shape[0], LANES - MLA_ROPE_DIM), F32)
    return (jnp.tile(c32, (1, HEADS)), jnp.tile(s32, (1, HEADS)),
            jnp.concatenate([c32, pad], axis=1), jnp.concatenate([s32, pad], axis=1))


def kernel(x_prompt, x_sample, cache_sb_k, cache_sb_v, cache_mla_ckv, cache_mla_krope, page_table, meta_tokens,
           ffn1_w_gu, ffn1_w_down, ln1_g, ln1_b, w_in, q_norm_g, w_uq, kv_norm_g, w_uk, w_uv, sb_out_g, mla_out_g,
           w_o, ln2_g, ln2_b, ffn2_w_gu, ffn2_w_down, ln3_g, ln3_b):
    assert w_in.shape[0] == DEPTH
    nb, seq, _ = x_prompt.shape
    nseq, dec_seq, _ = x_sample.shape
    ctx = seq + N_META
    lp = -(-ctx // ATT_BLOCK) * ATT_BLOCK
    past = page_table.shape[1] * PAGE_SIZE
    eye = jnp.eye(HEADS, dtype=F32)

    def ffn_weights(w_gu, w_down):
        return w_gu[0, :, :D_FF].astype(BF16), w_gu[0, :, D_FF:].astype(BF16), w_down[0].astype(BF16)

    ffn1 = ffn_weights(ffn1_w_gu, ffn1_w_down)
    ffn2 = ffn_weights(ffn2_w_gu, ffn2_w_down)
    win = jnp.pad(w_in[0], ((0, 0), (0, IN_PAD - IN_WIDTH))).astype(BF16)
    qk = MLA_NOPE_DIM + MLA_ROPE_DIM
    head_base = jnp.arange(HEADS)[:, None] * qk
    perm = jnp.concatenate([(head_base + jnp.arange(MLA_NOPE_DIM)[None, :]).reshape(-1),
                            (head_base + MLA_NOPE_DIM + jnp.arange(MLA_ROPE_DIM)[None, :]).reshape(-1)])
    wuq = w_uq[0][:, perm].astype(BF16)
    wuk_t = jnp.transpose(w_uk[0], (1, 2, 0))
    wuk_bd = (wuk_t[:, :, None, :] * eye[:, None, :, None]).reshape(SB_WIDTH, HEADS * MLA_KV_RANK).astype(BF16)
    wuv_t = jnp.transpose(w_uv[0], (1, 0, 2))
    wuv_bd = (wuv_t[:, :, None, :] * eye[:, None, :, None]).reshape(HEADS * MLA_KV_RANK, MLA_WIDTH).astype(BF16)
    wuv_flat = w_uv[0].reshape(MLA_KV_RANK, MLA_WIDTH).astype(BF16)
    sel_src = jnp.arange(HEADS * MLA_ROPE_DIM)
    sel_dst = (sel_src // MLA_ROPE_DIM) * LANES + sel_src % MLA_ROPE_DIM
    psel = jnp.zeros((HEADS * MLA_ROPE_DIM, HEADS * LANES), F32).at[sel_src, sel_dst].set(1.0).astype(BF16)
    wo_sb, wo_mla = w_o[0, :SB_WIDTH].astype(BF16), w_o[0, SB_WIDTH:].astype(BF16)

    meta = jnp.broadcast_to(meta_tokens[None].astype(x_prompt.dtype), (nb, N_META, D_MODEL))
    xp = jnp.concatenate([meta, x_prompt, jnp.zeros((nb, lp - ctx, D_MODEL), x_prompt.dtype)], axis=1)
    xp = xp.reshape(nb * lp, D_MODEL)
    xs = x_sample.reshape(nseq * dec_seq, D_MODEL)
    tab_p = _rope_tables(jnp.arange(lp))
    tab_s = _rope_tables(jnp.tile(past + jnp.arange(dec_seq), nseq))

    def front(x, batches, kv_positions, tabs):
        h1 = _ffn_ln(x, *ffn1, ln1_g, ln1_b)
        return h1, _proj(h1, batches, kv_positions, win, q_norm_g, wuq, kv_norm_g, wuk_bd, psel, *tabs)

    def back(h1, osb, omla):
        return _mix_ffn(h1, osb, omla, sb_out_g, mla_out_g, wo_sb, wo_mla, ln2_g, ln2_b, *ffn2, ln3_g, ln3_b)

    h1p, (kp, vp, cp, rp, qsb_p, ksb_p, vsb_p, qm_p, kcat_p) = front(xp, nb, ctx, tab_p)
    b3 = lambda a: a.reshape(nb, lp, a.shape[-1])
    osb_p = _sb_prompt(b3(qsb_p), b3(ksb_p), b3(vsb_p))
    omla_p = _mla_prompt(b3(qm_p), b3(kcat_p), wuv_bd)
    yp = back(h1p, osb_p.reshape(nb * lp, SB_WIDTH), omla_p.reshape(nb * lp, MLA_WIDTH))

    h1s, (ks, vs, cs, rs, qsb_s, _, _, qm_s, _) = front(xs, 1, None, tab_s)
    head_of_lane = jnp.arange(SB_WIDTH) // SB_HEAD_DIM
    head_mask = (head_of_lane[None, :] == jnp.arange(HEADS)[:, None]).astype(F32)
    qbd = (qsb_s.astype(F32).reshape(nseq, dec_seq, 1, SB_WIDTH) * head_mask[None, None]).reshape(
        nseq, dec_seq * HEADS, SB_WIDTH)
    qm4 = qm_s.astype(F32).reshape(nseq, dec_seq * HEADS, QM_SLOT)
    qlat_s = qm4[:, :, :MLA_KV_RANK]
    qrope_s = qm4[:, :, MLA_KV_RANK:MLA_KV_RANK + MLA_ROPE_DIM]
    new8 = lambda a: jnp.pad(a.reshape(nseq, dec_seq, a.shape[-1]), ((0, 0), (0, 8 - dec_seq), (0, 0)))
    feature_major = lambda c: jnp.moveaxis(c[0], 1, -1).reshape(c.shape[1], -1, PAGE_SIZE)
    osb_s, omla_s = _decode(
        page_table, qbd, new8(ks), new8(vs), qlat_s, qrope_s, new8(cs), new8(rs), wuv_flat,
        feature_major(cache_sb_k), feature_major(cache_sb_v), cache_mla_ckv[0], feature_major(cache_mla_krope))
    ys = back(h1s, osb_s.reshape(nseq * dec_seq, SB_WIDTH), omla_s.reshape(nseq * dec_seq, MLA_WIDTH))

    prompt = lambda a, shape: a.reshape((nb, lp) + shape)[:, :ctx][None]
    sample = lambda a, shape: a.reshape((nseq, dec_seq) + shape)[None]
    hd = (HEADS, SB_HEAD_DIM)
    prompt_kv = lambda a: jnp.moveaxis(a.reshape((nb,) + hd + (ctx,)), -1, 1)[None]
    return (yp.reshape(nb, lp, D_MODEL)[:, N_META:ctx], ys.reshape(nseq, dec_seq, D_MODEL),
            prompt_kv(kp), prompt_kv(vp), prompt(cp, (MLA_KV_RANK,)), prompt(rp, (MLA_ROPE_DIM,)),
            sample(ks, hd), sample(vs, hd), sample(cs, (MLA_KV_RANK,)), sample(rs, (MLA_ROPE_DIM,)))
```

```python
import functools

import jax
import jax.numpy as jnp
from jax import lax
from jax.experimental import pallas as pl
from jax.experimental.pallas import tpu as pltpu

F32 = jnp.float32
BF16 = jnp.bfloat16

D_MODEL = 1024
N_META = 16
HEADS = 8
SB_HEAD_DIM = 64
SB_WIDTH = HEADS * SB_HEAD_DIM
MLA_NOPE_DIM = 64
MLA_ROPE_DIM = 32
MLA_V_DIM = 64
MLA_WIDTH = HEADS * MLA_V_DIM
MLA_Q_RANK = 256
MLA_KV_RANK = 128
D_FF = 2816
PAGE_SIZE = 128
ROPE_BASE = 10000.0
LN_EPS = 1e-5
RMS_EPS = 1e-6
DEPTH = 1
DEEPNORM_ALPHA = (2 * DEPTH) ** 0.25
LOG2E = 1.4426950408889634
SB_QSCALE = SB_HEAD_DIM ** -0.5 * LOG2E
MLA_QSCALE = (MLA_NOPE_DIM + MLA_ROPE_DIM) ** -0.5 * LOG2E
IN_WIDTH = 3 * SB_WIDTH + MLA_Q_RANK + MLA_KV_RANK + MLA_ROPE_DIM

LANES = 128
MXU_DIM = 256
IN_PAD = 2048
QM_SLOT = 256
ONES_LANE = MLA_KV_RANK + MLA_ROPE_DIM
NEG_BIG = -1e30

ROW_TILE = 512
FF_CHUNK = 256
ATT_BLOCK = 256
SB_CHAIN_ROWS = 2048
SB_DEAD_LOG2 = -160.0
PAGES_PER_STEP = 16
PAGE_RING = 3
SAMPLE_ROWS = HEADS * 4
ROWWISE_VMEM = 56 * 1024 * 1024
ATTN_VMEM = 48 * 1024 * 1024


def _const_spec(shape):
    zeros = (0,) * len(shape)
    return pl.BlockSpec(shape, lambda *_: zeros, pipeline_mode=pl.Buffered(1))


def _layer_norm(y, g, b):
    mu = jnp.mean(y, axis=-1, keepdims=True)
    d = y - mu
    var = jnp.mean(d * d, axis=-1, keepdims=True)
    return d * lax.rsqrt(var + LN_EPS) * g + b


def _rms_norm(y, g):
    return y * lax.rsqrt(jnp.mean(y * y, axis=-1, keepdims=True) + RMS_EPS) * g


def _dot(a, b):
    return jnp.dot(a, b, preferred_element_type=F32)


def _dot_nt(a, b):
    return lax.dot_general(a, b, (((1,), (1,)), ((), ())), preferred_element_type=F32)


def _swiglu(xb, wg_ref, wu_ref, wd_ref, h_ref):
    for c in range(0, D_FF, FF_CHUNK):
        g = _dot(xb, wg_ref[:, c:c + FF_CHUNK])
        u = _dot(xb, wu_ref[:, c:c + FF_CHUNK])
        h_ref[:, c:c + FF_CHUNK] = (g * jax.nn.sigmoid(g) * u).astype(BF16)
    return _dot(h_ref[...], wd_ref[...])


def _rope_lanes(x, c, s):
    lane = lax.broadcasted_iota(jnp.int32, x.shape, 1)
    swapped = jnp.where((lane & 31) < 16, pltpu.roll(x, LANES - 16, 1), pltpu.roll(x, 16, 1))
    return x * c + swapped * s


def _ffn_ln_kernel(x_ref, wg_ref, wu_ref, wd_ref, g_ref, b_ref, o_ref, h_ref):
    x = x_ref[...]
    f = _swiglu(x.astype(BF16), wg_ref, wu_ref, wd_ref, h_ref)
    o_ref[...] = _layer_norm(DEEPNORM_ALPHA * x + 0.5 * f, g_ref[...], b_ref[...])


def _ffn_ln(x, wg, wu, wd, g, b):
    rows = x.shape[0]
    tm = min(ROW_TILE, rows)
    return pl.pallas_call(
        _ffn_ln_kernel,
        grid=(rows // tm,),
        in_specs=[pl.BlockSpec((tm, D_MODEL), lambda i: (i, 0)),
                  _const_spec((D_MODEL, D_FF)), _const_spec((D_MODEL, D_FF)), _const_spec((D_FF, D_MODEL)),
                  _const_spec((1, D_MODEL)), _const_spec((1, D_MODEL))],
        out_specs=pl.BlockSpec((tm, D_MODEL), lambda i: (i, 0)),
        out_shape=jax.ShapeDtypeStruct((rows, D_MODEL), F32),
        scratch_shapes=[pltpu.VMEM((tm, D_FF), BF16)],
        compiler_params=pltpu.CompilerParams(dimension_semantics=("parallel",), vmem_limit_bytes=ROWWISE_VMEM),
        name="ffn_ln",
    )(x, wg, wu, wd, g, b)


def _proj_kernel(h_ref, win_ref, qg_ref, wuq_ref, kvg_ref, wuk_ref, psel_ref, cq_ref, sq_ref, ck_ref, sk_ref,
                 k_out, v_out, ckv_out, kr_out, qsb_out, ksb_out, vsb_out, qm_out, kcat_out, *, kv_position_minor):
    p = _dot(h_ref[...].astype(BF16), win_ref[...])
    sq = p[:, 0:SB_WIDTH]
    sk = p[:, SB_WIDTH:2 * SB_WIDTH]
    sv = p[:, 2 * SB_WIDTH:3 * SB_WIDTH]
    cq = p[:, 3 * SB_WIDTH:3 * SB_WIDTH + MLA_Q_RANK]
    ckv = p[:, 3 * SB_WIDTH + MLA_Q_RANK:IN_PAD - LANES]
    krp = p[:, IN_PAD - LANES:]
    k_out[...] = sk.T if kv_position_minor else sk
    v_out[...] = sv.T if kv_position_minor else sv
    qsb_out[...] = (sq * SB_QSCALE).astype(BF16)
    ksb_out[...] = sk.astype(BF16)
    vsb_out[...] = sv.astype(BF16)

    q = _dot(_rms_norm(cq, qg_ref[...]).astype(BF16), wuq_ref[...])
    qlat = _dot(q[:, :SB_WIDTH].astype(BF16), wuk_ref[...]) * MLA_QSCALE
    qr = q[:, SB_WIDTH:] * MLA_QSCALE
    qro = jnp.concatenate(
        [_rope_lanes(qr[:, :LANES], cq_ref[:, :LANES], sq_ref[:, :LANES]),
         _rope_lanes(qr[:, LANES:], cq_ref[:, LANES:], sq_ref[:, LANES:])], axis=1)
    qrs = _dot(qro.astype(BF16), psel_ref[...])
    for h in range(HEADS):
        qm_out[:, h * QM_SLOT:h * QM_SLOT + LANES] = qlat[:, h * LANES:(h + 1) * LANES].astype(BF16)
        qm_out[:, h * QM_SLOT + LANES:(h + 1) * QM_SLOT] = qrs[:, h * LANES:(h + 1) * LANES].astype(BF16)

    ckvn = _rms_norm(ckv, kvg_ref[...])
    ckv_out[...] = ckvn
    kro = _rope_lanes(krp, ck_ref[...], sk_ref[...])
    kr_out[...] = kro[:, :MLA_ROPE_DIM]
    kcat_out[:, :LANES] = ckvn.astype(BF16)
    lane = lax.broadcasted_iota(jnp.int32, kro.shape, 1)
    kcat_out[:, LANES:] = jnp.where(lane == ONES_LANE - LANES, 1.0, kro).astype(BF16)


def _proj(h, nb, kv_positions, win, qg, wuq, kvg, wuk_bd, psel, cq, sq, ck, sk):
    rows = h.shape[0]
    tm = min(ATT_BLOCK, rows)
    nblk = rows // nb // tm
    row = lambda w: pl.BlockSpec((tm, w), lambda b, i: (b * nblk + i, 0))
    tab = lambda w: pl.BlockSpec((tm, w), lambda b, i: (i, 0))
    widths_dtypes = [(MLA_KV_RANK, F32), (MLA_ROPE_DIM, F32), (SB_WIDTH, BF16), (SB_WIDTH, BF16), (SB_WIDTH, BF16),
                     (HEADS * QM_SLOT, BF16), (2 * LANES, BF16)]
    if kv_positions is None:
        kv_spec, kv_shape = row(SB_WIDTH), jax.ShapeDtypeStruct((rows, SB_WIDTH), F32)
    else:
        kv_spec = pl.BlockSpec((None, SB_WIDTH, tm), lambda b, i: (b, 0, i))
        kv_shape = jax.ShapeDtypeStruct((nb, SB_WIDTH, kv_positions), F32)
    return pl.pallas_call(
        functools.partial(_proj_kernel, kv_position_minor=kv_positions is not None),
        grid=(nb, nblk),
        in_specs=[row(D_MODEL), _const_spec((D_MODEL, IN_PAD)), _const_spec((1, MLA_Q_RANK)),
                  _const_spec((MLA_Q_RANK, HEADS * (MLA_NOPE_DIM + MLA_ROPE_DIM))), _const_spec((1, MLA_KV_RANK)),
                  _const_spec((SB_WIDTH, HEADS * LANES)), _const_spec((2 * LANES, HEADS * LANES)),
                  tab(2 * LANES), tab(2 * LANES), tab(LANES), tab(LANES)],
        out_specs=[kv_spec, kv_spec] + [row(w) for w, _ in widths_dtypes],
        out_shape=[kv_shape, kv_shape] + [jax.ShapeDtypeStruct((rows, w), d) for w, d in widths_dtypes],
        compiler_params=pltpu.CompilerParams(dimension_semantics=("parallel", "parallel"),
                                             vmem_limit_bytes=ROWWISE_VMEM),
        name="mixer_proj",
    )(h, win, qg, wuq, kvg, wuk_bd, psel, cq, sq, ck, sk)


def _mix_ffn_kernel(h_ref, osb_ref, omla_ref, sbg_ref, mlag_ref, wo_sb_ref, wo_mla_ref, g2_ref, b2_ref,
                    wg_ref, wu_ref, wd_ref, g3_ref, b3_ref, o_ref, hs_ref):
    a = _rms_norm(osb_ref[...], sbg_ref[...]).astype(BF16)
    b = _rms_norm(omla_ref[...], mlag_ref[...]).astype(BF16)
    mixed = _dot(a, wo_sb_ref[...]) + _dot(b, wo_mla_ref[...])
    h2 = _layer_norm(DEEPNORM_ALPHA * h_ref[...] + mixed, g2_ref[...], b2_ref[...])
    f = _swiglu(h2.astype(BF16), wg_ref, wu_ref, wd_ref, hs_ref)
    o_ref[...] = _layer_norm(DEEPNORM_ALPHA * h2 + 0.5 * f, g3_ref[...], b3_ref[...])


def _mix_ffn(h, osb, omla, sbg, mlag, wo_sb, wo_mla, g2, b2, wg, wu, wd, g3, b3):
    rows = h.shape[0]
    tm = min(ROW_TILE, rows)
    row = lambda w: pl.BlockSpec((tm, w), lambda i: (i, 0))
    vec = lambda w: _const_spec((1, w))
    return pl.pallas_call(
        _mix_ffn_kernel,
        grid=(rows // tm,),
        in_specs=[row(D_MODEL), row(SB_WIDTH), row(MLA_WIDTH), vec(SB_WIDTH), vec(MLA_WIDTH),
                  _const_spec((SB_WIDTH, D_MODEL)), _const_spec((MLA_WIDTH, D_MODEL)), vec(D_MODEL), vec(D_MODEL),
                  _const_spec((D_MODEL, D_FF)), _const_spec((D_MODEL, D_FF)), _const_spec((D_FF, D_MODEL)),
                  vec(D_MODEL), vec(D_MODEL)],
        out_specs=row(D_MODEL),
        out_shape=jax.ShapeDtypeStruct((rows, D_MODEL), F32),
        scratch_shapes=[pltpu.VMEM((tm, D_FF), BF16)],
        compiler_params=pltpu.CompilerParams(dimension_semantics=("parallel",), vmem_limit_bytes=ROWWISE_VMEM),
        name="mix_ffn",
    )(h, osb, omla, sbg, mlag, wo_sb, wo_mla, g2, b2, wg, wu, wd, g3, b3)


def _later_key_matrix(n):
    r = lax.broadcasted_iota(jnp.int32, (n, n), 0)
    c = lax.broadcasted_iota(jnp.int32, (n, n), 1)
    return jnp.where(r > c, 1.0, 0.0).astype(BF16)


def _stick_breaking_terms(z, visible):
    t = jnp.log(1.0 + jnp.exp2(-jnp.abs(z))) * LOG2E
    log_beta = jnp.minimum(z, 0.0) - t
    log_keep = log_beta - z
    if visible is not None:
        log_keep = jnp.where(visible, log_keep, 0.0)
    hi = log_keep.astype(BF16)
    lo = (log_keep - hi.astype(F32)).astype(BF16)
    return log_beta, log_keep, hi, lo


def _lane_tiled(x, width):
    return jnp.concatenate([x] * (width // LANES), axis=1) if width > LANES else x


def _sb_prompt_kernel(q_ref, k_ref, v_ref, o_ref, qh_ref, acc_ref, car_ref, *, blk):
    qi = pl.program_id(1)
    pairs = SB_WIDTH // LANES
    pair_rows = 2 * blk
    lane = lax.broadcasted_iota(jnp.int32, (blk, LANES), 1)
    for hp in range(pairs):
        q2 = q_ref[:, hp * LANES:(hp + 1) * LANES]
        qh_ref[pl.ds(hp * pair_rows, blk)] = jnp.where(lane < SB_HEAD_DIM, q2, jnp.zeros_like(q2))
        qh_ref[pl.ds(hp * pair_rows + blk, blk)] = jnp.where(lane >= SB_HEAD_DIM, q2, jnp.zeros_like(q2))
    later = _later_key_matrix(blk)
    later2 = jnp.concatenate([later, later], axis=0)
    rowq = lax.broadcasted_iota(jnp.int32, (blk, blk), 0)
    colk = lax.broadcasted_iota(jnp.int32, (blk, blk), 1)
    acc_ref[...] = jnp.zeros_like(acc_ref)
    car_ref[...] = jnp.zeros_like(car_ref)
    chain_pairs = SB_CHAIN_ROWS // pair_rows

    def block(kj, visible):
        off = pl.multiple_of(kj * blk, blk)
        if visible is not None:
            visible = jnp.concatenate([visible] * (SB_CHAIN_ROWS // blk), axis=0)
        for c in range(pairs // chain_pairs):
            chain = range(c * chain_pairs, (c + 1) * chain_pairs)
            rows = pl.ds(c * SB_CHAIN_ROWS, SB_CHAIN_ROWS)
            z = jnp.concatenate(
                [_dot_nt(qh_ref[pl.ds(hp * pair_rows, pair_rows)], k_ref[pl.ds(off, blk), hp * LANES:(hp + 1) * LANES])
                 for hp in chain], axis=0)
            log_beta, log_keep, hi, lo = _stick_breaking_terms(z, visible)
            carry = car_ref[rows]
            after = _dot(jnp.concatenate([hi, lo], axis=1), later2) + _lane_tiled(carry, blk)
            w = jnp.exp2(log_beta + after)
            if visible is not None:
                w = jnp.where(visible, w, 0.0)
            wb = w.astype(BF16)
            for n, hp in enumerate(chain):
                acc_ref[pl.ds(hp * pair_rows, pair_rows)] += _dot(
                    wb[n * pair_rows:(n + 1) * pair_rows], v_ref[pl.ds(off, blk), hp * LANES:(hp + 1) * LANES])
            car_ref[rows] = carry + jnp.sum(log_keep, axis=-1, keepdims=True)

    block(qi, colk < rowq)

    def body(state):
        i, _ = state
        block(qi - 1 - i, None)
        return i + 1, (jnp.max(car_ref[...]) > SB_DEAD_LOG2).astype(jnp.int32)

    lax.while_loop(lambda state: (state[0] < qi) & (state[1] > 0), body, (jnp.int32(0), jnp.int32(1)))
    for hp in range(pairs):
        o_ref[:, hp * LANES:(hp + 1) * LANES] = jnp.where(
            lane < SB_HEAD_DIM, acc_ref[pl.ds(hp * pair_rows, blk)], acc_ref[pl.ds(hp * pair_rows + blk, blk)])


def _sb_prompt(q, k, v):
    nb, lp, _ = q.shape
    blk = ATT_BLOCK
    resident = pl.BlockSpec((None, lp, SB_WIDTH), lambda b, i: (b, 0, 0), pipeline_mode=pl.Buffered(1))
    return pl.pallas_call(
        functools.partial(_sb_prompt_kernel, blk=blk),
        grid=(nb, lp // blk),
        in_specs=[pl.BlockSpec((None, blk, SB_WIDTH), lambda b, i: (b, i, 0)), resident, resident],
        out_specs=pl.BlockSpec((None, blk, SB_WIDTH), lambda b, i: (b, i, 0)),
        out_shape=jax.ShapeDtypeStruct((nb, lp, SB_WIDTH), F32),
        scratch_shapes=[pltpu.VMEM((HEADS * blk, LANES), BF16), pltpu.VMEM((HEADS * blk, LANES), F32),
                        pltpu.VMEM((HEADS * blk, LANES), F32)],
        compiler_params=pltpu.CompilerParams(dimension_semantics=("parallel", "parallel"),
                                             vmem_limit_bytes=ATTN_VMEM),
        name="sb_prompt",
    )(q, k, v)


def _mla_prompt_kernel(q_ref, kc_ref, wuv_ref, o_ref, m_ref, acc_ref, *, blk):
    qi = pl.program_id(1)
    rowq = lax.broadcasted_iota(jnp.int32, (blk, blk), 0)
    colk = lax.broadcasted_iota(jnp.int32, (blk, blk), 1)
    m_ref[...] = jnp.full_like(m_ref, NEG_BIG)
    acc_ref[...] = jnp.zeros_like(acc_ref)

    def block(off, nkeys, visible):
        kc = kc_ref[pl.ds(pl.multiple_of(off, blk), nkeys), :]
        for h in range(HEADS):
            s = _dot_nt(q_ref[:, h * QM_SLOT:(h + 1) * QM_SLOT], kc)
            if visible is not None:
                s = jnp.where(visible, s, NEG_BIG)
            m_old = m_ref[h]
            m_new = jnp.maximum(m_old, jnp.max(s, axis=-1, keepdims=True))
            p = jnp.exp2(s - _lane_tiled(m_new, nkeys))
            acc_ref[h] = _lane_tiled(jnp.exp2(m_old - m_new), QM_SLOT) * acc_ref[h] + _dot(p.astype(BF16), kc)
            m_ref[h] = m_new

    block(qi * blk, blk, colk <= rowq)

    def body(i, c):
        block(i * (2 * blk), 2 * blk, None)
        return c

    lax.fori_loop(0, lax.shift_right_logical(qi, 1), body, 0)

    @pl.when((qi & 1) == 1)
    def _():
        block((qi - 1) * blk, blk, None)

    out = jnp.zeros((blk, MLA_WIDTH), F32)
    for h in range(HEADS):
        a = acc_ref[h]
        o_lat = (a[:, :MLA_KV_RANK] / a[:, ONES_LANE:ONES_LANE + 1]).astype(BF16)
        out = out + _dot(o_lat, wuv_ref[h * MLA_KV_RANK:(h + 1) * MLA_KV_RANK, :])
    o_ref[...] = out


def _mla_prompt(qm, kcat, wuv_bd):
    nb, lp, _ = qm.shape
    blk = ATT_BLOCK
    return pl.pallas_call(
        functools.partial(_mla_prompt_kernel, blk=blk),
        grid=(nb, lp // blk),
        in_specs=[pl.BlockSpec((None, blk, HEADS * QM_SLOT), lambda b, i: (b, i, 0)),
                  pl.BlockSpec((None, lp, QM_SLOT), lambda b, i: (b, 0, 0)),
                  _const_spec((HEADS * MLA_KV_RANK, MLA_WIDTH))],
        out_specs=pl.BlockSpec((None, blk, MLA_WIDTH), lambda b, i: (b, i, 0)),
        out_shape=jax.ShapeDtypeStruct((nb, lp, MLA_WIDTH), F32),
        scratch_shapes=[pltpu.VMEM((HEADS, blk, LANES), F32), pltpu.VMEM((HEADS, blk, QM_SLOT), F32)],
        compiler_params=pltpu.CompilerParams(dimension_semantics=("parallel", "parallel"),
                                             vmem_limit_bytes=ATTN_VMEM),
        name="mla_prompt",
    )(qm, kcat, wuv_bd)


def _decode_kernel(pt_ref, qbd_ref, kn_ref, vn_ref, qlat_ref, qrope_ref, cn_ref, rn_ref, wuv_ref,
                   k_hbm, v_hbm, c_hbm, r_hbm, osb_ref, omla_ref,
                   acc_ref, car_ref, m_ref, l_ref, macc_ref, k_buf, v_buf, c_buf, r_buf, sem,
                   alive_ref, kv_fetched_ref, *, ppg):
    step = pl.program_id(1)
    nsteps = pl.num_programs(1)
    flat = pl.program_id(0) * nsteps + step
    total = pl.num_programs(0) * nsteps
    slot = lax.rem(flat, PAGE_RING)
    caches = ((k_hbm, k_buf), (v_hbm, v_buf), (c_hbm, c_buf), (r_hbm, r_buf))
    kv, latent_rope = (0, 1), (2, 3)

    @pl.when(step == 0)
    def _():
        alive_ref[0] = 1

    def page_copy(f, sl, n, p):
        hbm, buf = caches[n]
        return pltpu.make_async_copy(hbm.at[pt_ref[f * ppg + p]], buf.at[sl, p], sem.at[sl, n])

    def start_fetch(f, sl, which):
        for n in which:
            for p in range(ppg):
                page_copy(f, sl, n, p).start()

    def wait_fetch(which):
        for n in which:
            for p in range(ppg):
                page_copy(flat, slot, n, p).wait()

    @pl.when(flat == 0)
    def _():
        for ahead in range(PAGE_RING - 1):
            start_fetch(ahead, ahead, kv + latent_rope)
            kv_fetched_ref[ahead] = 1

    ahead = flat + (PAGE_RING - 1)
    ahead_slot = lax.rem(ahead, PAGE_RING)

    @pl.when(ahead < total)
    def _():
        start_fetch(ahead, ahead_slot, latent_rope)

    wait_fetch(latent_rope)

    @pl.when(kv_fetched_ref[slot] == 1)
    def _():
        wait_fetch(kv)

    k_pages, v_pages, c_pages, r_pages = ([buf.at[slot, p] for p in range(ppg)] for _, buf in caches)
    rows = SAMPLE_ROWS
    qbd = qbd_ref[...].astype(BF16)
    qlat = qlat_ref[...].astype(BF16)
    qrope = qrope_ref[...].astype(BF16)
    later = _later_key_matrix(PAGE_SIZE)
    later2 = jnp.concatenate([later, later], axis=0)

    def sb_update(z, values, values_transposed, visible):
        n = z.shape[1] // PAGE_SIZE
        page = lambda x, p: x[:, p * PAGE_SIZE:(p + 1) * PAGE_SIZE]
        log_beta, log_keep, hi, lo = _stick_breaking_terms(z, visible)
        stacked = jnp.concatenate(
            [jnp.concatenate([page(hi, p), page(lo, p)], axis=1) for p in range(n)], axis=0)
        local = _dot(stacked, later2)
        carry = car_ref[...]
        after = [None] * n
        for p in reversed(range(n)):
            after[p] = local[p * rows:(p + 1) * rows] + carry
            carry = carry + jnp.sum(page(log_keep, p), axis=-1, keepdims=True)
        w = jnp.exp2(log_beta + jnp.concatenate(after, axis=1))
        if visible is not None:
            w = jnp.where(visible, w, 0.0)
        wb = w.astype(BF16)
        acc_ref[...] += _dot_nt(wb, values) if values_transposed else _dot(wb, values)
        car_ref[...] = carry

    def mla_update(s, latent, visible):
        if visible is not None:
            s = jnp.where(visible, s, NEG_BIG)
        m_old = m_ref[...]
        m_new = jnp.maximum(m_old, jnp.max(s, axis=-1, keepdims=True))
        alpha = jnp.exp2(m_old - m_new)
        p = jnp.exp2(s - m_new)
        l_ref[...] = alpha * l_ref[...] + jnp.sum(p, axis=-1, keepdims=True)
        macc_ref[...] = alpha * macc_ref[...] + _dot(p.astype(BF16), latent)
        m_ref[...] = m_new

    @pl.when(step == 0)
    def _():
        acc_ref[...] = jnp.zeros_like(acc_ref)
        car_ref[...] = jnp.zeros_like(car_ref)
        m_ref[...] = jnp.full_like(m_ref, NEG_BIG)
        l_ref[...] = jnp.zeros_like(l_ref)
        macc_ref[...] = jnp.zeros_like(macc_ref)
        q_idx = lax.broadcasted_iota(jnp.int32, (rows, PAGE_SIZE), 0) >> 3
        col = lax.broadcasted_iota(jnp.int32, (rows, PAGE_SIZE), 1)
        pad = lambda ref: jnp.concatenate(
            [ref[...], jnp.zeros((PAGE_SIZE - ref.shape[0], ref.shape[1]), F32)], axis=0).astype(BF16)
        sb_update(_dot_nt(qbd, pad(kn_ref)), pad(vn_ref), False, col < q_idx)
        latent = pad(cn_ref)
        mla_update(_dot_nt(qlat, latent) + _dot_nt(qrope, pad(rn_ref)), latent, col <= q_idx)

    gather = lambda refs, axis: jnp.concatenate([r[...].astype(BF16) for r in refs], axis=axis)

    @pl.when(alive_ref[0] == 1)
    def _():
        sb_update(_dot(qbd, gather(k_pages, 1)), gather(v_pages, 1), True, None)
        alive_ref[0] = (jnp.max(car_ref[...]) > SB_DEAD_LOG2).astype(jnp.int32)

    @pl.when(ahead < total)
    def _():
        dead = (step + (PAGE_RING - 1) < nsteps) & (alive_ref[0] == 0)
        kv_fetched_ref[ahead_slot] = jnp.where(dead, 0, 1)

        @pl.when(jnp.logical_not(dead))
        def _():
            start_fetch(ahead, ahead_slot, kv)

    latent = gather(c_pages, 0)
    mla_update(_dot_nt(qlat, latent) + _dot(qrope, gather(r_pages, 1)), latent, None)

    @pl.when(step == nsteps - 1)
    def _():
        r = lax.broadcasted_iota(jnp.int32, (rows, SB_WIDTH), 0)
        c = lax.broadcasted_iota(jnp.int32, (rows, SB_WIDTH), 1)
        own = (c >> 6) == (r & 7)
        fold = lambda x: jnp.sum(jnp.where(own, x, 0.0).reshape(rows // HEADS, HEADS, SB_WIDTH), axis=1)
        osb_ref[...] = fold(acc_ref[...])
        o_lat = (macc_ref[...] / l_ref[...]).astype(BF16)
        omla_ref[...] = fold(_dot(o_lat, wuv_ref[...]))


def _decode(page_table, qbd, kn, vn, qlat, qrope, cn, rn, wuv_flat, cache_k, cache_v, cache_c, cache_r):
    nseq, npages = page_table.shape
    ppg = PAGES_PER_STEP
    steps = npages // ppg

    order = page_table.reshape(nseq, steps, ppg)[:, ::-1, :].reshape(-1)
    caches = (cache_k, cache_v, cache_c, cache_r)
    per_seq = lambda r, w: pl.BlockSpec((None, r, w), lambda s, j, pt: (s, 0, 0))
    new_rows = kn.shape[1]
    in_specs = [per_seq(SAMPLE_ROWS, SB_WIDTH), per_seq(new_rows, SB_WIDTH), per_seq(new_rows, SB_WIDTH),
                per_seq(SAMPLE_ROWS, MLA_KV_RANK), per_seq(SAMPLE_ROWS, MLA_ROPE_DIM),
                per_seq(new_rows, MLA_KV_RANK), per_seq(new_rows, MLA_ROPE_DIM),
                pl.BlockSpec((MLA_KV_RANK, MLA_WIDTH), lambda s, j, pt: (0, 0))]
    in_specs += [pl.BlockSpec(memory_space=pl.ANY)] * len(caches)
    assert nseq * steps >= PAGE_RING - 1
    page_bufs = [pltpu.VMEM((PAGE_RING, ppg) + c.shape[1:], c.dtype) for c in caches]
    nq = SAMPLE_ROWS // HEADS
    out_spec = pl.BlockSpec((None, nq, SB_WIDTH), lambda s, j, pt: (s, 0, 0))
    return pl.pallas_call(
        functools.partial(_decode_kernel, ppg=ppg),
        grid_spec=pltpu.PrefetchScalarGridSpec(
            num_scalar_prefetch=1,
            grid=(nseq, steps),
            in_specs=in_specs,
            out_specs=[out_spec, out_spec],
            scratch_shapes=[pltpu.VMEM((SAMPLE_ROWS, SB_WIDTH), F32), pltpu.VMEM((SAMPLE_ROWS, 1), F32),
                            pltpu.VMEM((SAMPLE_ROWS, 1), F32), pltpu.VMEM((SAMPLE_ROWS, 1), F32),
                            pltpu.VMEM((SAMPLE_ROWS, MLA_KV_RANK), F32)]
            + page_bufs + [pltpu.SemaphoreType.DMA((PAGE_RING, len(caches))),
                           pltpu.SMEM((1,), jnp.int32), pltpu.SMEM((PAGE_RING,), jnp.int32)]),
        out_shape=[jax.ShapeDtypeStruct((nseq, nq, SB_WIDTH), F32), jax.ShapeDtypeStruct((nseq, nq, MLA_WIDTH), F32)],
        compiler_params=pltpu.CompilerParams(dimension_semantics=("arbitrary", "arbitrary"),
                                             vmem_limit_bytes=ATTN_VMEM),
        name="paged_decode",
    )(order, qbd, kn, vn, qlat, qrope, cn, rn, wuv_flat, *caches)


def _rope_tables(pos):
    inv = ROPE_BASE ** (-jnp.arange(0, MLA_ROPE_DIM, 2, dtype=F32) / MLA_ROPE_DIM)
    ang = pos.astype(F32)[:, None] * inv[None, :]
    cos, sin = jnp.cos(ang), jnp.sin(ang)
    c32 = jnp.concatenate([cos, cos], axis=1)
    s32 = jnp.concatenate([-sin, sin], axis=1)
    pad = jnp.zeros((pos.shape[0], LANES - MLA_ROPE_DIM), F32)
    return (jnp.tile(c32, (1, HEADS)), jnp.tile(s32, (1, HEADS)),
            jnp.concatenate([c32, pad], axis=1), jnp.concatenate([s32, pad], axis=1))


def kernel(x_prompt, x_sample, cache_sb_k, cache_sb_v, cache_mla_ckv, cache_mla_krope, page_table, meta_tokens,
           ffn1_w_gu, ffn1_w_down, ln1_g, ln1_b, w_in, q_norm_g, w_uq, kv_norm_g, w_uk, w_uv, sb_out_g, mla_out_g,
           w_o, ln2_g, ln2_b, ffn2_w_gu, ffn2_w_down, ln3_g, ln3_b):
    assert w_in.shape[0] == DEPTH
    nb, seq, _ = x_prompt.shape
    nseq, dec_seq, _ = x_sample.shape
    ctx = seq + N_META
    lp = -(-ctx // ATT_BLOCK) * ATT_BLOCK
    past = page_table.shape[1] * PAGE_SIZE
    eye = jnp.eye(HEADS, dtype=F32)

    def ffn_weights(w_gu, w_down):
        return w_gu[0, :, :D_FF].astype(BF16), w_gu[0, :, D_FF:].astype(BF16), w_down[0].astype(BF16)

    ffn1 = ffn_weights(ffn1_w_gu, ffn1_w_down)
    ffn2 = ffn_weights(ffn2_w_gu, ffn2_w_down)
    win = jnp.pad(w_in[0], ((0, 0), (0, IN_PAD - IN_WIDTH))).astype(BF16)
    qk = MLA_NOPE_DIM + MLA_ROPE_DIM
    head_base = jnp.arange(HEADS)[:, None] * qk
    perm = jnp.concatenate([(head_base + jnp.arange(MLA_NOPE_DIM)[None, :]).reshape(-1),
                            (head_base + MLA_NOPE_DIM + jnp.arange(MLA_ROPE_DIM)[None, :]).reshape(-1)])
    wuq = w_uq[0][:, perm].astype(BF16)
    wuk_t = jnp.transpose(w_uk[0], (1, 2, 0))
    wuk_bd = (wuk_t[:, :, None, :] * eye[:, None, :, None]).reshape(SB_WIDTH, HEADS * MLA_KV_RANK).astype(BF16)
    wuv_t = jnp.transpose(w_uv[0], (1, 0, 2))
    wuv_bd = (wuv_t[:, :, None, :] * eye[:, None, :, None]).reshape(HEADS * MLA_KV_RANK, MLA_WIDTH).astype(BF16)
    wuv_flat = w_uv[0].reshape(MLA_KV_RANK, MLA_WIDTH).astype(BF16)
    sel_src = jnp.arange(HEADS * MLA_ROPE_DIM)
    sel_dst = (sel_src // MLA_ROPE_DIM) * LANES + sel_src % MLA_ROPE_DIM
    psel = jnp.zeros((HEADS * MLA_ROPE_DIM, HEADS * LANES), F32).at[sel_src, sel_dst].set(1.0).astype(BF16)
    wo_sb, wo_mla = w_o[0, :SB_WIDTH].astype(BF16), w_o[0, SB_WIDTH:].astype(BF16)

    meta = jnp.broadcast_to(meta_tokens[None].astype(x_prompt.dtype), (nb, N_META, D_MODEL))
    xp = jnp.concatenate([meta, x_prompt, jnp.zeros((nb, lp - ctx, D_MODEL), x_prompt.dtype)], axis=1)
    xp = xp.reshape(nb * lp, D_MODEL)
    xs = x_sample.reshape(nseq * dec_seq, D_MODEL)
    tab_p = _rope_tables(jnp.arange(lp))
    tab_s = _rope_tables(jnp.tile(past + jnp.arange(dec_seq), nseq))

    def front(x, batches, kv_positions, tabs):
        h1 = _ffn_ln(x, *ffn1, ln1_g, ln1_b)
        return h1, _proj(h1, batches, kv_positions, win, q_norm_g, wuq, kv_norm_g, wuk_bd, psel, *tabs)

    def back(h1, osb, omla):
        return _mix_ffn(h1, osb, omla, sb_out_g, mla_out_g, wo_sb, wo_mla, ln2_g, ln2_b, *ffn2, ln3_g, ln3_b)

    h1p, (kp, vp, cp, rp, qsb_p, ksb_p, vsb_p, qm_p, kcat_p) = front(xp, nb, ctx, tab_p)
    b3 = lambda a: a.reshape(nb, lp, a.shape[-1])
    osb_p = _sb_prompt(b3(qsb_p), b3(ksb_p), b3(vsb_p))
    omla_p = _mla_prompt(b3(qm_p), b3(kcat_p), wuv_bd)
    yp = back(h1p, osb_p.reshape(nb * lp, SB_WIDTH), omla_p.reshape(nb * lp, MLA_WIDTH))

    h1s, (ks, vs, cs, rs, qsb_s, _, _, qm_s, _) = front(xs, 1, None, tab_s)
    head_of_lane = jnp.arange(SB_WIDTH) // SB_HEAD_DIM
    head_mask = (head_of_lane[None, :] == jnp.arange(HEADS)[:, None]).astype(F32)
    qbd = (qsb_s.astype(F32).reshape(nseq, dec_seq, 1, SB_WIDTH) * head_mask[None, None]).reshape(
        nseq, dec_seq * HEADS, SB_WIDTH)
    qm4 = qm_s.astype(F32).reshape(nseq, dec_seq * HEADS, QM_SLOT)
    qlat_s = qm4[:, :, :MLA_KV_RANK]
    qrope_s = qm4[:, :, MLA_KV_RANK:MLA_KV_RANK + MLA_ROPE_DIM]
    new8 = lambda a: jnp.pad(a.reshape(nseq, dec_seq, a.shape[-1]), ((0, 0), (0, 8 - dec_seq), (0, 0)))
    feature_major = lambda c: jnp.moveaxis(c[0], 1, -1).reshape(c.shape[1], -1, PAGE_SIZE)
    osb_s, omla_s = _decode(
        page_table, qbd, new8(ks), new8(vs), qlat_s, qrope_s, new8(cs), new8(rs), wuv_flat,
        feature_major(cache_sb_k), feature_major(cache_sb_v), cache_mla_ckv[0], feature_major(cache_mla_krope))
    ys = back(h1s, osb_s.reshape(nseq * dec_seq, SB_WIDTH), omla_s.reshape(nseq * dec_seq, MLA_WIDTH))

    prompt = lambda a, shape: a.reshape((nb, lp) + shape)[:, :ctx][None]
    sample = lambda a, shape: a.reshape((nseq, dec_seq) + shape)[None]
    hd = (HEADS, SB_HEAD_DIM)
    prompt_kv = lambda a: jnp.moveaxis(a.reshape((nb,) + hd + (ctx,)), -1, 1)[None]
    return (yp.reshape(nb, lp, D_MODEL)[:, N_META:ctx], ys.reshape(nseq, dec_seq, D_MODEL),
            prompt_kv(kp), prompt_kv(vp), prompt(cp, (MLA_KV_RANK,)), prompt(rp, (MLA_ROPE_DIM,)),
            sample(ks, hd), sample(vs, hd), sample(cs, (MLA_KV_RANK,)), sample(rs, (MLA_ROPE_DIM,)))
```
